```python
import math
import jax
import jax.numpy as jnp
from jax import lax
import numpy as np

D_MODEL = 1024
BATCH = 2
SEQ = 16384
DEPTH = 2

HEAD_DIM = 64
FOX_HEADS = 8
DIFF_HEADS = 4
DIFF_V_DIM = 2 * HEAD_DIM
FOX_WIDTH = FOX_HEADS * HEAD_DIM
DIFF_QK_WIDTH = DIFF_HEADS * 2 * HEAD_DIM
DIFF_WIDTH = DIFF_HEADS * DIFF_V_DIM
ATTN_WIDTH = FOX_WIDTH + DIFF_WIDTH
ATTN_SPLITS = (FOX_WIDTH, FOX_WIDTH, FOX_WIDTH, FOX_HEADS, DIFF_QK_WIDTH, DIFF_QK_WIDTH, DIFF_WIDTH, ATTN_WIDTH)
ATTN_IN = 3 * FOX_WIDTH + FOX_HEADS + 2 * DIFF_QK_WIDTH + DIFF_WIDTH + ATTN_WIDTH

ML_HEADS = 8
ML_QK_DIM = 64
ML_V_DIM = 128
ML_QK_WIDTH = ML_HEADS * ML_QK_DIM
ML_WIDTH = ML_HEADS * ML_V_DIM
ML_SPLITS = (ML_QK_WIDTH, ML_QK_WIDTH, ML_WIDTH, ML_HEADS, ML_HEADS, ML_WIDTH, ML_WIDTH)
ML_IN = 2 * ML_QK_WIDTH + 3 * ML_WIDTH + 2 * ML_HEADS

Q_BLOCK = 128
ML_CHUNK = 128
ROPE_THETA = 10000.0
EPS = 1e-6
POS_OFFSET_MAX = 4096
N_ATTN_LAYERS = (DEPTH + 1) // 2
N_MLSTM_LAYERS = DEPTH // 2

kernel_name = 'hybrid_fox_diff_mlstm'

F32 = jnp.float32


def rms_norm(x, g):
    xf = x.astype(F32)
    y = xf * lax.rsqrt(jnp.mean(xf * xf, axis=-1, keepdims=True) + EPS)
    return (y * g.astype(F32)).astype(x.dtype)


def split_cols(t, sizes):
    idx = np.cumsum(sizes)[:-1].tolist()
    return jnp.split(t, idx, axis=-1)


def rope(x, pos):
    half = x.shape[-1] // 2
    inv = ROPE_THETA ** (-jnp.arange(half, dtype=F32) / half)
    ang = pos.astype(F32)[..., None] * inv
    ang = ang.reshape((x.shape[0],) + (1,) * (x.ndim - 3) + ang.shape[1:])
    cos, sin = jnp.cos(ang), jnp.sin(ang)
    x1 = x[..., :half].astype(F32)
    x2 = x[..., half:].astype(F32)
    return jnp.concatenate([x1 * cos - x2 * sin, x1 * sin + x2 * cos], axis=-1).astype(x.dtype)


def seq_blocks(t, axis, size):
    n = t.shape[axis] // size
    t = t.reshape(t.shape[:axis] + (n, size) + t.shape[axis + 1:])
    return jnp.moveaxis(t, axis, 0)


def seq_unblocks(t, axis):
    t = jnp.moveaxis(t, 0, axis)
    return t.reshape(t.shape[:axis] + (-1,) + t.shape[axis + 2:])


def fox_attention(q, k, v, log_f):
    S = q.shape[2]
    scale = HEAD_DIM ** -0.5
    F = jnp.cumsum(log_f, axis=-1)
    key_pos = jnp.arange(S)
    nb = S // Q_BLOCK

    def block(args):
        i, qi, Fi = args
        q_pos = i * Q_BLOCK + jnp.arange(Q_BLOCK)
        s = jnp.einsum('bhqd,bhkd->bhqk', qi, k, preferred_element_type=F32) * scale
        s = s + Fi[..., :, None] - F[..., None, :]
        s = jnp.where(key_pos[None, :] <= q_pos[:, None], s, -jnp.inf)
        p = jax.nn.softmax(s, axis=-1)
        return jnp.einsum('bhqk,bhkd->bhqd', p.astype(v.dtype), v)

    out = lax.map(block, (jnp.arange(nb), seq_blocks(q, 2, Q_BLOCK), seq_blocks(F, 2, Q_BLOCK)))
    return seq_unblocks(out, 2)


def diff_attention(q, k, v, lam):
    S = q.shape[3]
    scale = HEAD_DIM ** -0.5
    key_pos = jnp.arange(S)
    nb = S // Q_BLOCK

    def block(args):
        i, qi = args
        q_pos = i * Q_BLOCK + jnp.arange(Q_BLOCK)
        s = jnp.einsum('bhcqd,bhckd->bhcqk', qi, k, preferred_element_type=F32) * scale
        s = jnp.where(key_pos[None, :] <= q_pos[:, None], s, -jnp.inf)
        p = jax.nn.softmax(s, axis=-1)
        pd = p[:, :, 0] - lam * p[:, :, 1]
        return jnp.einsum('bhqk,bhke->bhqe', pd.astype(v.dtype), v)

    out = lax.map(block, (jnp.arange(nb), seq_blocks(q, 3, Q_BLOCK)))
    return seq_unblocks(out, 2)


def mlstm_chunkwise(q, k, v, log_i, log_f):
    B, H, S, dk = q.shape
    dv = v.shape[-1]
    L = ML_CHUNK
    causal = jnp.tril(jnp.ones((L, L), dtype=bool))

    def step(carry, xs):
        C, n, m = carry
        qi, ki, vi, li, lf = xs
        qf, kf, vf = qi.astype(F32), ki.astype(F32), vi.astype(F32)
        b = jnp.cumsum(lf, axis=-1)
        a = b[..., -1]
        log_d = b[..., :, None] - b[..., None, :] + li[..., None, :]
        log_d = jnp.where(causal, log_d, -jnp.inf)
        log_inter = b + m[..., None]
        m_t = jnp.maximum(log_inter, jnp.max(log_d, axis=-1))
        d = jnp.exp(log_d - m_t[..., None])
        w_inter = jnp.exp(log_inter - m_t)
        s = jnp.einsum('bhtd,bhsd->bhts', qf, kf) * d
        num = w_inter[..., None] * jnp.einsum('bhtd,bhde->bhte', qf, C) + jnp.einsum('bhts,bhse->bhte', s, vf)
        den = w_inter * jnp.einsum('bhtd,bhd->bht', qf, n) + jnp.sum(s, axis=-1)
        h = num / jnp.maximum(jnp.abs(den), jnp.exp(-m_t))[..., None]
        g = a[..., None] - b + li
        m_new = jnp.maximum(a + m, jnp.max(g, axis=-1))
        w_old = jnp.exp(a + m - m_new)
        w_s = jnp.exp(g - m_new[..., None])
        C_new = w_old[..., None, None] * C + jnp.einsum('bhs,bhsd,bhse->bhde', w_s, kf, vf)
        n_new = w_old[..., None] * n + jnp.einsum('bhs,bhsd->bhd', w_s, kf)
        return (C_new, n_new, m_new), h.astype(v.dtype)

    init = (jnp.zeros((B, H, dk, dv), F32), jnp.zeros((B, H, dk), F32), jnp.zeros((B, H), F32))
    xs = (seq_blocks(q, 2, L), seq_blocks(k, 2, L), seq_blocks(v, 2, L),
          seq_blocks(log_i, 2, L), seq_blocks(log_f, 2, L))
    _, hs = lax.scan(step, init, xs)
    return seq_unblocks(hs, 2)


def attention_layer(x, pos, norm_g, w_in, b_f, lam_q1, lam_k1, lam_q2, lam_k2, subln_g, w_out, layer_idx):
    B, S, _ = x.shape
    h = rms_norm(x, norm_g)
    proj = jnp.einsum('bsd,de->bse', h, w_in)
    fq, fk, fv, ff, dq, dk, dv, z = split_cols(proj, ATTN_SPLITS)

    def heads(t, nh):
        return t.reshape(B, S, nh, -1).transpose(0, 2, 1, 3)

    log_f = jax.nn.log_sigmoid((ff + b_f).astype(F32)).transpose(0, 2, 1)
    o_fox = fox_attention(heads(fq, FOX_HEADS), heads(fk, FOX_HEADS), heads(fv, FOX_HEADS), log_f)

    dq = rope(dq.reshape(B, S, DIFF_HEADS, 2, HEAD_DIM).transpose(0, 2, 3, 1, 4), pos)
    dk = rope(dk.reshape(B, S, DIFF_HEADS, 2, HEAD_DIM).transpose(0, 2, 3, 1, 4), pos)
    lam_init = 0.8 - 0.6 * math.exp(-0.3 * layer_idx)
    lam = (jnp.exp(jnp.sum(lam_q1.astype(F32) * lam_k1.astype(F32)))
           - jnp.exp(jnp.sum(lam_q2.astype(F32) * lam_k2.astype(F32))) + lam_init)
    o_diff = diff_attention(dq, dk, heads(dv, DIFF_HEADS), lam)
    o_diff = rms_norm(o_diff, subln_g) * (1.0 - lam_init)

    y = jnp.concatenate([o_fox.transpose(0, 2, 1, 3).reshape(B, S, FOX_WIDTH),
                         o_diff.transpose(0, 2, 1, 3).reshape(B, S, DIFF_WIDTH)], axis=-1)
    y = y * jax.nn.silu(z)
    return x + jnp.einsum('bse,ed->bsd', y, w_out)


def mlstm_layer(x, norm_g, w_in, b_i, b_f, w_out):
    B, S, _ = x.shape
    h = rms_norm(x, norm_g)
    proj = jnp.einsum('bsd,de->bse', h, w_in)
    q, k, v, ig, fg, og, z = split_cols(proj, ML_SPLITS)

    def heads(t, nh):
        return t.reshape(B, S, nh, -1).transpose(0, 2, 1, 3)

    q = heads(q, ML_HEADS)
    k = heads(k, ML_HEADS) * (ML_QK_DIM ** -0.5)
    v = heads(v, ML_HEADS)
    log_i = (ig + b_i).astype(F32).transpose(0, 2, 1)
    log_f = jax.nn.log_sigmoid((fg + b_f).astype(F32)).transpose(0, 2, 1)
    hm = mlstm_chunkwise(q, k, v, log_i, log_f)
    hm = hm.transpose(0, 2, 1, 3).reshape(B, S, ML_WIDTH)
    y = jax.nn.sigmoid(og) * hm * jax.nn.silu(z)
    return x + jnp.einsum('bse,ed->bsd', y, w_out)


def setup_inputs(seed: int = 0) -> dict:
    key = jax.random.key(seed)
    ks = jax.random.split(key, 18)
    ne, no = N_ATTN_LAYERS, N_MLSTM_LAYERS
    nrm = jax.random.normal
    x = nrm(ks[0], (BATCH, SEQ, D_MODEL), F32)
    positions = (jnp.arange(SEQ, dtype=jnp.int32)[None, :]
                 + jax.random.randint(ks[1], (BATCH, 1), 0, POS_OFFSET_MAX, dtype=jnp.int32))
    attn_norm_g = 1.0 + 0.02 * nrm(ks[2], (ne, D_MODEL), F32)
    attn_w_in = nrm(ks[3], (ne, D_MODEL, ATTN_IN), F32) * D_MODEL ** -0.5
    fox_b_f = jax.random.uniform(ks[4], (ne, FOX_HEADS), F32, 1.0, 4.0)
    diff_lam_q1 = 0.1 * nrm(ks[5], (ne, HEAD_DIM), F32)
    diff_lam_k1 = 0.1 * nrm(ks[6], (ne, HEAD_DIM), F32)
    diff_lam_q2 = 0.1 * nrm(ks[7], (ne, HEAD_DIM), F32)
    diff_lam_k2 = 0.1 * nrm(ks[8], (ne, HEAD_DIM), F32)
    diff_subln_g = 1.0 + 0.02 * nrm(ks[9], (ne, DIFF_V_DIM), F32)
    attn_w_out = nrm(ks[10], (ne, ATTN_WIDTH, D_MODEL), F32) * ATTN_WIDTH ** -0.5
    ml_norm_g = 1.0 + 0.02 * nrm(ks[11], (no, D_MODEL), F32)
    ml_w_in = nrm(ks[12], (no, D_MODEL, ML_IN), F32) * D_MODEL ** -0.5
    ml_b_i = 0.1 * nrm(ks[13], (no, ML_HEADS), F32)
    ml_b_f = jax.random.uniform(ks[14], (no, ML_HEADS), F32, 3.0, 6.0)
    ml_w_out = nrm(ks[15], (no, ML_WIDTH, D_MODEL), F32) * ML_WIDTH ** -0.5
    final_norm_g = 1.0 + 0.02 * nrm(ks[16], (D_MODEL,), F32)
    return {'x': x, 'positions': positions, 'attn_norm_g': attn_norm_g, 'attn_w_in': attn_w_in,
            'fox_b_f': fox_b_f, 'diff_lam_q1': diff_lam_q1, 'diff_lam_k1': diff_lam_k1,
            'diff_lam_q2': diff_lam_q2, 'diff_lam_k2': diff_lam_k2, 'diff_subln_g': diff_subln_g,
            'attn_w_out': attn_w_out, 'ml_norm_g': ml_norm_g, 'ml_w_in': ml_w_in, 'ml_b_i': ml_b_i,
            'ml_b_f': ml_b_f, 'ml_w_out': ml_w_out, 'final_norm_g': final_norm_g}


def reference(x, positions, attn_norm_g, attn_w_in, fox_b_f, diff_lam_q1, diff_lam_k1, diff_lam_q2,
              diff_lam_k2, diff_subln_g, attn_w_out, ml_norm_g, ml_w_in, ml_b_i, ml_b_f, ml_w_out,
              final_norm_g):
    for layer in range(DEPTH):
        j = layer // 2
        if layer % 2 == 0:
            x = attention_layer(x, positions, attn_norm_g[j], attn_w_in[j], fox_b_f[j],
                                diff_lam_q1[j], diff_lam_k1[j], diff_lam_q2[j], diff_lam_k2[j],
                                diff_subln_g[j], attn_w_out[j], layer)
        else:
            x = mlstm_layer(x, ml_norm_g[j], ml_w_in[j], ml_b_i[j], ml_b_f[j], ml_w_out[j])
    return rms_norm(x, final_norm_g)
```

```python
import functools
import math

import numpy as np
import jax
import jax.numpy as jnp
from jax import lax
from jax.experimental import pallas as pl
from jax.experimental.pallas import tpu as pltpu

F32 = jnp.float32
BF16 = jnp.bfloat16

D_MODEL = 1024
HEAD_DIM = 64
FOX_HEADS = 8
DIFF_HEADS = 4
FOX_WIDTH = FOX_HEADS * HEAD_DIM
DIFF_WIDTH = DIFF_HEADS * 2 * HEAD_DIM
ATTN_WIDTH = FOX_WIDTH + DIFF_WIDTH
ML_HEADS = 8
ML_QK_DIM = 64
ML_V_DIM = 128
ML_QK_WIDTH = ML_HEADS * ML_QK_DIM
ML_WIDTH = ML_HEADS * ML_V_DIM
ROPE_THETA = 10000.0
EPS = 1e-6

LANES = 128
N_UNITS = 4
ATTN_BLOCK = 512
ML_CHUNK = 128
ROW_BLOCK = 512
VMEM_LIMIT = 56 * 1024 * 1024

NEG_INF = float("-inf")


def _split3(x):
    hi = x.astype(BF16)
    r1 = x - hi.astype(F32)
    mid = r1.astype(BF16)
    lo = (r1 - mid.astype(F32)).astype(BF16)
    return hi, mid, lo


def _tri_cumsum(x):
    n = x.shape[0]
    row = lax.broadcasted_iota(jnp.int32, (n, n), 0)
    col = lax.broadcasted_iota(jnp.int32, (n, n), 1)
    tri = jnp.where(row >= col, 1.0, 0.0).astype(BF16)
    hi, mid, lo = _split3(x)
    return (jnp.dot(tri, hi, preferred_element_type=F32)
            + jnp.dot(tri, mid, preferred_element_type=F32)
            + jnp.dot(tri, lo, preferred_element_type=F32))


def _log_sigmoid(x):
    return jnp.minimum(x, 0.0) - jnp.log(1.0 + jnp.exp(-jnp.abs(x)))


def _sigmoid(x):
    return 1.0 / (1.0 + jnp.exp(-x))


def _rms(x, g):
    return x * lax.rsqrt(jnp.mean(x * x, axis=-1, keepdims=True) + EPS) * g


def _attn_proj_kernel(x_ref, pos_ref, g_ref, w_ref, wf_ref, bf_ref, inv_ref,
                      qf_ref, kf_ref, vft_ref, f_ref, qd_ref, kd_ref, vdt_ref, sz_ref,
                      carry_ref):
    @pl.when(pl.program_id(1) == 0)
    def _():
        carry_ref[...] = jnp.zeros_like(carry_ref)

    hb = _rms(x_ref[...], g_ref[...]).astype(BF16)

    def proj(lo, hi):
        return jnp.dot(hb, w_ref[:, lo:hi], preferred_element_type=F32)

    qf_ref[...] = proj(0, 512).astype(BF16)
    kf_ref[...] = proj(512, 1024).astype(BF16)
    vf = proj(1024, 1536)
    for u in range(N_UNITS):
        vft_ref[u] = vf[:, u * LANES:(u + 1) * LANES].T.astype(BF16)

    ang = pos_ref[...] * inv_ref[...]
    cos = jnp.cos(ang)
    sin = jnp.sin(ang)
    lane = lax.broadcasted_iota(jnp.int32, (1, LANES), 1)
    sin_signed = jnp.where(lane < 64, -sin, sin)

    def rope_store(lo, out_ref):
        t = proj(lo, lo + 512)
        for h in range(DIFF_HEADS):
            xh = t[:, h * LANES:(h + 1) * LANES]
            out_ref[:, h * LANES:(h + 1) * LANES] = (
                xh * cos + pltpu.roll(xh, 64, 1) * sin_signed).astype(BF16)

    rope_store(1536, qd_ref)
    rope_store(2048, kd_ref)
    vd = proj(2560, 3072)
    for u in range(N_UNITS):
        vdt_ref[u] = vd[:, u * LANES:(u + 1) * LANES].T.astype(BF16)

    z = proj(3072, 4096)
    sz_ref[...] = (z * _sigmoid(z)).astype(BF16)

    lf = _log_sigmoid(jnp.dot(hb, wf_ref[...], preferred_element_type=F32) + bf_ref[...])
    cs = _tri_cumsum(lf) + carry_ref[...]
    carry_ref[...] = cs[ROW_BLOCK - 1:ROW_BLOCK, :]
    f_ref[...] = cs[:, :FOX_HEADS]


def _attn_proj(x, posf, g, w, wf, bf, inv):
    B, S, _ = x.shape
    T = ROW_BLOCK
    nb = S // T
    row = lambda b, i: (b, i, 0)
    const2 = lambda b, i: (0, 0)
    vt_spec = pl.BlockSpec((None, N_UNITS, None, LANES, T), lambda b, i: (b, 0, i, 0, 0))
    out_shape = (
        jax.ShapeDtypeStruct((B, S, 512), BF16),
        jax.ShapeDtypeStruct((B, S, 512), BF16),
        jax.ShapeDtypeStruct((B, N_UNITS, nb, LANES, T), BF16),
        jax.ShapeDtypeStruct((B, S, FOX_HEADS), F32),
        jax.ShapeDtypeStruct((B, S, 512), BF16),
        jax.ShapeDtypeStruct((B, S, 512), BF16),
        jax.ShapeDtypeStruct((B, N_UNITS, nb, LANES, T), BF16),
        jax.ShapeDtypeStruct((B, S, ATTN_WIDTH), BF16),
    )
    return pl.pallas_call(
        _attn_proj_kernel,
        grid=(B, nb),
        in_specs=[
            pl.BlockSpec((None, T, D_MODEL), row),
            pl.BlockSpec((None, T, 1), row),
            pl.BlockSpec((1, D_MODEL), const2),
            pl.BlockSpec((D_MODEL, 4096), const2),
            pl.BlockSpec((D_MODEL, LANES), const2),
            pl.BlockSpec((1, LANES), const2),
            pl.BlockSpec((1, LANES), const2),
        ],
        out_specs=(
            pl.BlockSpec((None, T, 512), row),
            pl.BlockSpec((None, T, 512), row),
            vt_spec,
            pl.BlockSpec((None, T, FOX_HEADS), row),
            pl.BlockSpec((None, T, 512), row),
            pl.BlockSpec((None, T, 512), row),
            vt_spec,
            pl.BlockSpec((None, T, ATTN_WIDTH), row),
        ),
        out_shape=out_shape,
        scratch_shapes=[pltpu.VMEM((1, LANES), F32)],
        compiler_params=pltpu.CompilerParams(
            dimension_semantics=("arbitrary", "arbitrary"), vmem_limit_bytes=VMEM_LIMIT),
        name="attn_proj",
    )(x, posf, g, w, wf, bf, inv)


def _attention_kernel(*refs, fox, lam_init):
    if fox:
        q_ref, k_ref, vt_ref, f_ref, sz_ref, o_ref, acc_ref = refs
    else:
        q_ref, k_ref, vt_ref, lam_ref, g_ref, sz_ref, o_ref, acc_ref = refs
    T = ATTN_BLOCK
    qi = pl.program_id(2)

    q = q_ref[...]
    lane = lax.broadcasted_iota(jnp.int32, (1, LANES), 1)
    first = (lane < 64) if fox else ((lane % 64) < 32)
    zero = jnp.zeros_like(q)
    qm = (jnp.where(first, q, zero), jnp.where(first, zero, q))

    acc_ref[...] = jnp.zeros_like(acc_ref)
    if fox:
        q0 = pl.multiple_of(qi * T, T)
        f0 = [f_ref[pl.ds(q0, 1), c:c + 1] for c in range(2)]

    def step(kb, carry, diag):
        start = pl.multiple_of(kb * T, T)
        k = k_ref[pl.ds(start, T), :]
        vt = vt_ref[kb]
        if diag:
            kpos = lax.broadcasted_iota(jnp.int32, (T, T), 0)
            qpos = lax.broadcasted_iota(jnp.int32, (T, T), 1)
            allowed = kpos <= qpos
        out = []
        for c in range(2):
            m_prev, l_prev = carry[2 * c], carry[2 * c + 1]
            s = lax.dot_general(k, qm[c], (((1,), (1,)), ((), ())),
                                preferred_element_type=F32)
            if fox:
                s = s - (f_ref[pl.ds(start, T), c:c + 1] - f0[c])
            if diag:
                s = jnp.where(allowed, s, NEG_INF)
            m_new = jnp.maximum(m_prev, jnp.max(s, axis=0, keepdims=True))
            alpha = jnp.exp(m_prev - m_new)
            p = jnp.exp(s - m_new)
            l_new = alpha * l_prev + jnp.sum(p, axis=0, keepdims=True)
            pb = p.astype(BF16)
            if fox:
                rows = slice(c * 64, (c + 1) * 64)
                acc_ref[rows, :] = alpha * acc_ref[rows, :] + jnp.dot(
                    vt[rows, :], pb, preferred_element_type=F32)
            else:
                acc_ref[c] = alpha * acc_ref[c] + jnp.dot(vt, pb, preferred_element_type=F32)
            out += [m_new, l_new]
        return tuple(out)

    init = (jnp.full((1, T), NEG_INF, F32), jnp.zeros((1, T), F32)) * 2
    carry = lax.fori_loop(0, qi, lambda kb, cr: step(kb, cr, False), init)
    m0, l0, m1, l1 = step(qi, carry, True)

    sz = sz_ref[...].astype(F32)
    if fox:
        o_t = jnp.concatenate([acc_ref[0:64, :] / l0, acc_ref[64:128, :] / l1], axis=0)
        o_ref[...] = (o_t.T * sz).astype(BF16)
    else:
        lam = (jnp.exp(jnp.sum(lam_ref[0:1, :] * lam_ref[1:2, :], axis=1, keepdims=True))
               - jnp.exp(jnp.sum(lam_ref[2:3, :] * lam_ref[3:4, :], axis=1, keepdims=True))
               + lam_init)
        o_t = acc_ref[0] / l0 - lam * (acc_ref[1] / l1)
        var = jnp.mean(o_t * o_t, axis=0, keepdims=True)
        y_t = o_t * lax.rsqrt(var + EPS) * g_ref[...] * (1.0 - lam_init)
        o_ref[...] = (y_t.T * sz).astype(BF16)


def _attention(q, k, vt, sz, *, fox, f=None, lam_vecs=None, subln_g=None, lam_init=0.0, sz_offset=0):
    B, S, _ = q.shape
    T = ATTN_BLOCK
    nq = S // T
    qspec = pl.BlockSpec((None, T, LANES), lambda b, u, i: (b, i, u))
    kspec = pl.BlockSpec((None, S, LANES), lambda b, u, i: (b, 0, u))
    vspec = pl.BlockSpec((None, None, nq, LANES, T), lambda b, u, i: (b, u, 0, 0, 0))
    szspec = pl.BlockSpec((None, T, LANES), lambda b, u, i: (b, i, u + sz_offset))
    if fox:
        extra = [f]
        extra_specs = [pl.BlockSpec((None, None, S, 2), lambda b, u, i: (b, u, 0, 0))]
        scratch = [pltpu.VMEM((LANES, T), F32)]
    else:
        extra = [lam_vecs, subln_g]
        extra_specs = [pl.BlockSpec((4, HEAD_DIM), lambda b, u, i: (0, 0)),
                       pl.BlockSpec((LANES, 1), lambda b, u, i: (0, 0))]
        scratch = [pltpu.VMEM((2, LANES, T), F32)]
    return pl.pallas_call(
        functools.partial(_attention_kernel, fox=fox, lam_init=lam_init),
        grid=(B, N_UNITS, nq),
        in_specs=[qspec, kspec, vspec] + extra_specs + [szspec],
        out_specs=pl.BlockSpec((None, T, LANES), lambda b, u, i: (b, i, u)),
        out_shape=jax.ShapeDtypeStruct((B, S, N_UNITS * LANES), BF16),
        scratch_shapes=scratch,
        compiler_params=pltpu.CompilerParams(
            dimension_semantics=("arbitrary", "arbitrary", "arbitrary"),
            vmem_limit_bytes=VMEM_LIMIT),
        name="fox_attention" if fox else "diff_attention",
    )(q, k, vt, *extra, sz)


def _mid_proj_kernel(x_ref, yf_ref, yd_ref, wo_ref, g_ref, w_ref, wg_ref, bg_ref,
                     x1_ref, q_ref, kt_ref, v_ref, gates_ref, gate_ref):
    x1 = (x_ref[...]
          + jnp.dot(yf_ref[...], wo_ref[0:FOX_WIDTH, :], preferred_element_type=F32)
          + jnp.dot(yd_ref[...], wo_ref[FOX_WIDTH:ATTN_WIDTH, :], preferred_element_type=F32))
    x1_ref[...] = x1
    hb = _rms(x1, g_ref[...]).astype(BF16)

    def proj(lo, hi):
        return jnp.dot(hb, w_ref[:, lo:hi], preferred_element_type=F32)

    q_ref[...] = proj(0, 512).astype(BF16)
    kt_ref[...] = proj(512, 1024).T.astype(BF16)
    v_ref[...] = proj(1024, 2048).astype(BF16)
    og = proj(2048, 3072)
    z = proj(3072, 4096)
    gate_ref[...] = (_sigmoid(og) * (z * _sigmoid(z))).astype(BF16)

    gt = jnp.dot(hb, wg_ref[...], preferred_element_type=F32) + bg_ref[...]
    lane = lax.broadcasted_iota(jnp.int32, (1, LANES), 1)
    gates_ref[...] = jnp.where(lane < ML_HEADS, gt, _log_sigmoid(gt))


def _mid_proj(x, yf, yd, wo, g, w, wg, bg):
    B, S, _ = x.shape
    T = ROW_BLOCK
    row = lambda b, i: (b, i, 0)
    const2 = lambda b, i: (0, 0)
    return pl.pallas_call(
        _mid_proj_kernel,
        grid=(B, S // T),
        in_specs=[
            pl.BlockSpec((None, T, D_MODEL), row),
            pl.BlockSpec((None, T, FOX_WIDTH), row),
            pl.BlockSpec((None, T, DIFF_WIDTH), row),
            pl.BlockSpec((ATTN_WIDTH, D_MODEL), const2),
            pl.BlockSpec((1, D_MODEL), const2),
            pl.BlockSpec((D_MODEL, 4096), const2),
            pl.BlockSpec((D_MODEL, LANES), const2),
            pl.BlockSpec((1, LANES), const2),
        ],
        out_specs=(
            pl.BlockSpec((None, T, D_MODEL), row),
            pl.BlockSpec((None, T, ML_QK_WIDTH), row),
            pl.BlockSpec((None, ML_QK_WIDTH, T), lambda b, i: (b, 0, i)),
            pl.BlockSpec((None, T, ML_WIDTH), row),
            pl.BlockSpec((None, T, LANES), row),
            pl.BlockSpec((None, T, ML_WIDTH), row),
        ),
        out_shape=(
            jax.ShapeDtypeStruct((B, S, D_MODEL), F32),
            jax.ShapeDtypeStruct((B, S, ML_QK_WIDTH), BF16),
            jax.ShapeDtypeStruct((B, ML_QK_WIDTH, S), BF16),
            jax.ShapeDtypeStruct((B, S, ML_WIDTH), BF16),
            jax.ShapeDtypeStruct((B, S, LANES), F32),
            jax.ShapeDtypeStruct((B, S, ML_WIDTH), BF16),
        ),
        compiler_params=pltpu.CompilerParams(
            dimension_semantics=("arbitrary", "arbitrary"), vmem_limit_bytes=VMEM_LIMIT),
        name="mid_proj",
    )(x, yf, yd, wo, g, w, wg, bg)


def _mlstm_kernel(q_ref, kt_ref, v_ref, gates_ref, gate_ref, o_ref, c_ref, m_ref):
    L = ML_CHUNK

    @pl.when(pl.program_id(1) == 0)
    def _():
        c_ref[...] = jnp.zeros_like(c_ref)
        m_ref[...] = jnp.zeros_like(m_ref)

    gts = gates_ref[...]
    lane = lax.broadcasted_iota(jnp.int32, (1, LANES), 1)
    cols = jnp.where(lane < ML_HEADS, gts, _tri_cumsum(gts))
    rows = cols.T
    causal = (lax.broadcasted_iota(jnp.int32, (L, L), 0)
              >= lax.broadcasted_iota(jnp.int32, (L, L), 1))
    ones_col = jnp.where(lane == 0, 1.0, 0.0).astype(BF16) + jnp.zeros((L, LANES), BF16)

    for h in range(ML_HEADS):
        li_row = rows[h:h + 1, :]
        b_row = rows[ML_HEADS + h:ML_HEADS + h + 1, :]
        b_col = cols[:, ML_HEADS + h:ML_HEADS + h + 1]
        a = b_col[L - 1:L, :]
        m_prev = m_ref[:, h:h + 1]

        log_d = jnp.where(causal, b_col - b_row + li_row, NEG_INF)
        log_inter = b_col + m_prev
        m_t = jnp.maximum(log_inter, jnp.max(log_d, axis=1, keepdims=True))
        d = jnp.exp(log_d - m_t)
        w_inter = jnp.exp(log_inter - m_t)

        qh = q_ref[:, h * ML_QK_DIM:(h + 1) * ML_QK_DIM]
        kth = kt_ref[h * ML_QK_DIM:(h + 1) * ML_QK_DIM, :]
        vh = v_ref[:, h * ML_V_DIM:(h + 1) * ML_V_DIM]
        vext = jnp.concatenate([vh, ones_col], axis=1)
        cext = c_ref[h]

        s = jnp.dot(qh, kth, preferred_element_type=F32) * d
        sv = jnp.dot(s.astype(BF16), vh, preferred_element_type=F32)
        qc = jnp.dot(qh, cext.astype(BF16), preferred_element_type=F32)
        num = w_inter * qc[:, :ML_V_DIM] + sv
        den = w_inter * qc[:, ML_V_DIM:ML_V_DIM + 1] + jnp.sum(s, axis=1, keepdims=True)
        hh = num / jnp.maximum(jnp.abs(den), jnp.exp(-m_t))

        g_row = a - b_row + li_row
        m_new = jnp.maximum(a + m_prev, jnp.max(g_row, axis=1, keepdims=True))
        w_old = jnp.exp(a + m_prev - m_new)
        kw = (kth.astype(F32) * jnp.exp(g_row - m_new)).astype(BF16)
        c_ref[h] = w_old * cext + jnp.dot(kw, vext, preferred_element_type=F32)
        m_ref[:, h:h + 1] = m_new

        sl = slice(h * ML_V_DIM, (h + 1) * ML_V_DIM)
        o_ref[:, sl] = (hh * gate_ref[:, sl].astype(F32)).astype(BF16)


def _mlstm(q, kt, v, gates, gate):
    B, S, _ = q.shape
    L = ML_CHUNK
    row = lambda b, c: (b, c, 0)
    return pl.pallas_call(
        _mlstm_kernel,
        grid=(B, S // L),
        in_specs=[
            pl.BlockSpec((None, L, ML_QK_WIDTH), row),
            pl.BlockSpec((None, ML_QK_WIDTH, L), lambda b, c: (b, 0, c)),
            pl.BlockSpec((None, L, ML_WIDTH), row),
            pl.BlockSpec((None, L, LANES), row),
            pl.BlockSpec((None, L, ML_WIDTH), row),
        ],
        out_specs=pl.BlockSpec((None, L, ML_WIDTH), row),
        out_shape=jax.ShapeDtypeStruct((B, S, ML_WIDTH), BF16),
        scratch_shapes=[pltpu.VMEM((ML_HEADS, ML_QK_DIM, 2 * ML_V_DIM), F32),
                        pltpu.VMEM((1, LANES), F32)],
        compiler_params=pltpu.CompilerParams(
            dimension_semantics=("arbitrary", "arbitrary"), vmem_limit_bytes=VMEM_LIMIT),
        name="mlstm",
    )(q, kt, v, gates, gate)


def _final_proj_kernel(x_ref, y_ref, wo_ref, g_ref, o_ref):
    x2 = x_ref[...] + jnp.dot(y_ref[...], wo_ref[...], preferred_element_type=F32)
    o_ref[...] = _rms(x2, g_ref[...])


def _final_proj(x1, y, wo, g):
    B, S, _ = x1.shape
    T = ROW_BLOCK
    row = lambda b, i: (b, i, 0)
    const2 = lambda b, i: (0, 0)
    return pl.pallas_call(
        _final_proj_kernel,
        grid=(B, S // T),
        in_specs=[
            pl.BlockSpec((None, T, D_MODEL), row),
            pl.BlockSpec((None, T, ML_WIDTH), row),
            pl.BlockSpec((ML_WIDTH, D_MODEL), const2),
            pl.BlockSpec((1, D_MODEL), const2),
        ],
        out_specs=pl.BlockSpec((None, T, D_MODEL), row),
        out_shape=jax.ShapeDtypeStruct((B, S, D_MODEL), F32),
        compiler_params=pltpu.CompilerParams(
            dimension_semantics=("arbitrary", "arbitrary"), vmem_limit_bytes=VMEM_LIMIT),
        name="final_proj",
    )(x1, y, wo, g)


def _rope_lane_perm():
    n = np.arange(LANES)
    half, c, r = n // 64, (n % 64) // 32, n % 32
    per_head = c * HEAD_DIM + half * 32 + r
    return np.concatenate([h * LANES + per_head for h in range(DIFF_HEADS)])


def _pad_lanes(t):
    return jnp.pad(t, ((0, 0), (0, LANES - t.shape[1])))


def kernel(x, positions, attn_norm_g, attn_w_in, fox_b_f, diff_lam_q1, diff_lam_k1, diff_lam_q2,
           diff_lam_k2, diff_subln_g, attn_w_out, ml_norm_g, ml_w_in, ml_b_i, ml_b_f, ml_w_out,
           final_norm_g):
    B, S, _ = x.shape
    scale = HEAD_DIM ** -0.5

    w = attn_w_in[0]
    o = np.cumsum((0, FOX_WIDTH, FOX_WIDTH, FOX_WIDTH, FOX_HEADS, DIFF_WIDTH, DIFF_WIDTH, DIFF_WIDTH,
                   ATTN_WIDTH))
    perm = _rope_lane_perm()
    w_main = jnp.concatenate([
        w[:, o[0]:o[1]] * scale, w[:, o[1]:o[2]], w[:, o[2]:o[3]],
        w[:, o[4]:o[5]][:, perm] * scale, w[:, o[5]:o[6]][:, perm], w[:, o[6]:o[7]],
        w[:, o[7]:o[8]]], axis=1).astype(BF16)
    w_f = _pad_lanes(w[:, o[3]:o[4]]).astype(BF16)
    b_f = _pad_lanes(fox_b_f[0][None, :])
    half = HEAD_DIM // 2
    inv = ROPE_THETA ** (-jnp.arange(half, dtype=F32) / half)
    inv = jnp.tile(inv, LANES // half)[None, :]
    posf = positions.astype(F32)[..., None]

    qf, kf, vft, fcum, qd, kd, vdt, sz = _attn_proj(
        x, posf, attn_norm_g[0][None, :], w_main, w_f, b_f, inv)

    f_units = fcum.reshape(B, S, N_UNITS, 2).transpose(0, 2, 1, 3)
    y_fox = _attention(qf, kf, vft, sz, fox=True, f=f_units, sz_offset=0)
    lam_vecs = jnp.stack([diff_lam_q1[0], diff_lam_k1[0], diff_lam_q2[0], diff_lam_k2[0]])
    lam_init = 0.8 - 0.6 * math.exp(-0.3 * 0)
    y_diff = _attention(qd, kd, vdt, sz, fox=False, lam_vecs=lam_vecs,
                        subln_g=diff_subln_g[0][:, None], lam_init=lam_init, sz_offset=N_UNITS)

    w2 = ml_w_in[0]
    o2 = np.cumsum((0, ML_QK_WIDTH, ML_QK_WIDTH, ML_WIDTH, ML_HEADS, ML_HEADS, ML_WIDTH, ML_WIDTH))
    w2_main = jnp.concatenate([
        w2[:, o2[0]:o2[1]], w2[:, o2[1]:o2[2]] * (ML_QK_DIM ** -0.5), w2[:, o2[2]:o2[3]],
        w2[:, o2[5]:o2[6]], w2[:, o2[6]:o2[7]]], axis=1).astype(BF16)
    w2_g = _pad_lanes(w2[:, o2[3]:o2[5]]).astype(BF16)
    b_g = _pad_lanes(jnp.concatenate([ml_b_i[0], ml_b_f[0]])[None, :])

    x1, q2, k2t, v2, gates, gate = _mid_proj(
        x, y_fox, y_diff, attn_w_out[0].astype(BF16), ml_norm_g[0][None, :], w2_main, w2_g, b_g)
    y2 = _mlstm(q2, k2t, v2, gates, gate)
    return _final_proj(x1, y2, ml_w_out[0].astype(BF16), final_norm_g[None, :])
```

```python
import functools
import math

import numpy as np
import jax
import jax.numpy as jnp
from jax import lax
from jax.experimental import pallas as pl
from jax.experimental.pallas import tpu as pltpu

F32 = jnp.float32
BF16 = jnp.bfloat16

D_MODEL = 1024
HEAD_DIM = 64
FOX_HEADS = 8
DIFF_HEADS = 4
FOX_WIDTH = FOX_HEADS * HEAD_DIM
DIFF_WIDTH = DIFF_HEADS * 2 * HEAD_DIM
ATTN_WIDTH = FOX_WIDTH + DIFF_WIDTH
ML_HEADS = 8
ML_QK_DIM = 64
ML_V_DIM = 128
ML_QK_WIDTH = ML_HEADS * ML_QK_DIM
ML_WIDTH = ML_HEADS * ML_V_DIM
ROPE_THETA = 10000.0
EPS = 1e-6

LANES = 128
N_UNITS = 4
ATTN_BLOCK = 512
Q_SUB = 256
ONES_ROWS = 16
FOX_MAP_ROWS = HEAD_DIM + ONES_ROWS
DIFF_MAP_ROWS = 2 * HEAD_DIM + ONES_ROWS
LOG2E = math.log2(math.e)
ML_CHUNK = 128
ROW_BLOCK = 512
VMEM_LIMIT = 56 * 1024 * 1024

NEG_INF = float("-inf")


def _split3(x):
    hi = x.astype(BF16)
    r1 = x - hi.astype(F32)
    mid = r1.astype(BF16)
    lo = (r1 - mid.astype(F32)).astype(BF16)
    return hi, mid, lo


def _tri_cumsum(x):
    n = x.shape[0]
    row = lax.broadcasted_iota(jnp.int32, (n, n), 0)
    col = lax.broadcasted_iota(jnp.int32, (n, n), 1)
    tri = jnp.where(row >= col, 1.0, 0.0).astype(BF16)
    hi, mid, lo = _split3(x)
    return (jnp.dot(tri, hi, preferred_element_type=F32)
            + jnp.dot(tri, mid, preferred_element_type=F32)
            + jnp.dot(tri, lo, preferred_element_type=F32))


def _log_sigmoid(x):
    return jnp.minimum(x, 0.0) - jnp.log(1.0 + jnp.exp(-jnp.abs(x)))


def _sigmoid(x):
    return 1.0 / (1.0 + jnp.exp(-x))


def _rms(x, g):
    return x * lax.rsqrt(jnp.mean(x * x, axis=-1, keepdims=True) + EPS) * g


def _attn_proj_kernel(x_ref, pos_ref, g_ref, w_ref, wf_ref, bf_ref, inv_ref,
                      qf_ref, kf_ref, vft_ref, f_ref, qd_ref, kd_ref, vdt_ref, sz_ref,
                      carry_ref):
    @pl.when(pl.program_id(1) == 0)
    def _():
        carry_ref[...] = jnp.zeros_like(carry_ref)

    hb = _rms(x_ref[...], g_ref[...]).astype(BF16)

    def proj(lo, hi):
        return jnp.dot(hb, w_ref[:, lo:hi], preferred_element_type=F32)

    qf_ref[...] = proj(0, 512).astype(BF16)
    kf_ref[...] = proj(512, 1024).astype(BF16)
    ones = jnp.ones((ONES_ROWS, ROW_BLOCK), BF16)
    vf = proj(1024, 1536)
    for u in range(N_UNITS):
        t = vf[:, u * LANES:(u + 1) * LANES].T.astype(BF16)
        for c in range(2):
            base = c * FOX_MAP_ROWS
            vft_ref[u, base:base + HEAD_DIM, :] = t[c * HEAD_DIM:(c + 1) * HEAD_DIM, :]
            vft_ref[u, base + HEAD_DIM:base + FOX_MAP_ROWS, :] = ones

    ang = pos_ref[...] * inv_ref[...]
    cos = jnp.cos(ang)
    sin = jnp.sin(ang)
    lane = lax.broadcasted_iota(jnp.int32, (1, LANES), 1)
    sin_signed = jnp.where(lane < 64, -sin, sin)

    def rope_store(lo, out_ref):
        t = proj(lo, lo + 512)
        for h in range(DIFF_HEADS):
            xh = t[:, h * LANES:(h + 1) * LANES]
            out_ref[:, h * LANES:(h + 1) * LANES] = (
                xh * cos + pltpu.roll(xh, 64, 1) * sin_signed).astype(BF16)

    rope_store(1536, qd_ref)
    rope_store(2048, kd_ref)
    vd = proj(2560, 3072)
    for u in range(N_UNITS):
        vdt_ref[u, 0:LANES, :] = vd[:, u * LANES:(u + 1) * LANES].T.astype(BF16)
        vdt_ref[u, LANES:DIFF_MAP_ROWS, :] = ones

    z = proj(3072, 4096)
    sz_ref[...] = (z * _sigmoid(z)).astype(BF16)

    lf = _log_sigmoid(jnp.dot(hb, wf_ref[...], preferred_element_type=F32) + bf_ref[...])
    cs = _tri_cumsum(lf) + carry_ref[...]
    carry_ref[...] = cs[ROW_BLOCK - 1:ROW_BLOCK, :]
    f_ref[...] = cs[:, :FOX_HEADS] * LOG2E


def _attn_proj(x, posf, g, w, wf, bf, inv):
    B, S, _ = x.shape
    T = ROW_BLOCK
    nb = S // T
    row = lambda b, i: (b, i, 0)
    const2 = lambda b, i: (0, 0)
    vt_spec = lambda rows: pl.BlockSpec((None, N_UNITS, None, rows, T), lambda b, i: (b, 0, i, 0, 0))
    out_shape = (
        jax.ShapeDtypeStruct((B, S, 512), BF16),
        jax.ShapeDtypeStruct((B, S, 512), BF16),
        jax.ShapeDtypeStruct((B, N_UNITS, nb, 2 * FOX_MAP_ROWS, T), BF16),
        jax.ShapeDtypeStruct((B, S, FOX_HEADS), F32),
        jax.ShapeDtypeStruct((B, S, 512), BF16),
        jax.ShapeDtypeStruct((B, S, 512), BF16),
        jax.ShapeDtypeStruct((B, N_UNITS, nb, DIFF_MAP_ROWS, T), BF16),
        jax.ShapeDtypeStruct((B, S, ATTN_WIDTH), BF16),
    )
    return pl.pallas_call(
        _attn_proj_kernel,
        grid=(B, nb),
        in_specs=[
            pl.BlockSpec((None, T, D_MODEL), row),
            pl.BlockSpec((None, T, 1), row),
            pl.BlockSpec((1, D_MODEL), const2),
            pl.BlockSpec((D_MODEL, 4096), const2),
            pl.BlockSpec((D_MODEL, LANES), const2),
            pl.BlockSpec((1, LANES), const2),
            pl.BlockSpec((1, LANES), const2),
        ],
        out_specs=(
            pl.BlockSpec((None, T, 512), row),
            pl.BlockSpec((None, T, 512), row),
            vt_spec(2 * FOX_MAP_ROWS),
            pl.BlockSpec((None, T, FOX_HEADS), row),
            pl.BlockSpec((None, T, 512), row),
            pl.BlockSpec((None, T, 512), row),
            vt_spec(DIFF_MAP_ROWS),
            pl.BlockSpec((None, T, ATTN_WIDTH), row),
        ),
        out_shape=out_shape,
        scratch_shapes=[pltpu.VMEM((1, LANES), F32)],
        compiler_params=pltpu.CompilerParams(
            dimension_semantics=("arbitrary", "arbitrary"), vmem_limit_bytes=VMEM_LIMIT),
        name="attn_proj",
    )(x, posf, g, w, wf, bf, inv)


def _attention_kernel(*refs, fox, lam_init):
    if fox:
        q_ref, k_ref, vt_ref, f_ref, sz_ref, o_ref, acc_ref, sa_ref, sb_ref = refs
    else:
        q_ref, k_ref, vt_ref, lam_ref, g_ref, sz_ref, o_ref, acc_ref, sa_ref, sb_ref = refs
    T = ATTN_BLOCK
    vrows = FOX_MAP_ROWS if fox else DIFF_MAP_ROWS
    vdim = vrows - ONES_ROWS
    qi = pl.program_id(2)

    q = q_ref[...]
    lane = lax.broadcasted_iota(jnp.int32, (1, LANES), 1)
    first = (lane < 64) if fox else ((lane % 64) < 32)
    zero = jnp.zeros_like(q)
    qm = (jnp.where(first, q, zero), jnp.where(first, zero, q))
    qm = [[qc[j * Q_SUB:(j + 1) * Q_SUB, :] for j in range(T // Q_SUB)] for qc in qm]

    acc_ref[...] = jnp.zeros_like(acc_ref)
    if fox:
        q0 = pl.multiple_of(qi * T, T)
        f0 = [f_ref[pl.ds(q0, 1), c:c + 1] for c in range(2)]

    chains = [(c, j) for c in range(2) for j in range(T // Q_SUB)]

    def scores(kb, s_ref, diag):
        start = pl.multiple_of(kb * T, T)
        k = k_ref[pl.ds(start, T), :]
        if fox:
            bias = [f_ref[pl.ds(start, T), c:c + 1] - f0[c] for c in range(2)]
        if diag:
            kpos = lax.broadcasted_iota(jnp.int32, (T, Q_SUB), 0)
            qpos = lax.broadcasted_iota(jnp.int32, (T, Q_SUB), 1)
        for c, j in chains:
            sc = lax.dot_general(k, qm[c][j], (((1,), (1,)), ((), ())),
                                 preferred_element_type=F32)
            if fox:
                sc = sc - bias[c]
            if diag:
                sc = jnp.where(kpos <= qpos + j * Q_SUB, sc, NEG_INF)
            s_ref[c, :, j * Q_SUB:(j + 1) * Q_SUB] = sc
            yield jnp.max(sc, axis=0, keepdims=True)

    def accumulate(kb, s_ref, mb, m):
        for i, (c, j) in enumerate(chains):
            lanes = slice(j * Q_SUB, (j + 1) * Q_SUB)
            m_new = jnp.maximum(m[i], mb[i])
            alpha = jnp.exp2(m[i] - m_new)
            pb = jnp.exp2(s_ref[c, :, lanes] - m_new).astype(BF16)
            vt = vt_ref[kb, 0:vrows, :] if not fox else vt_ref[kb, c * vrows:(c + 1) * vrows, :]
            acc_ref[c, :, lanes] = alpha * acc_ref[c, :, lanes] + jnp.dot(
                vt, pb, preferred_element_type=F32)
            yield m_new

    def overlapped(kb_next, s_next, kb_cur, s_cur, mb_cur, m):
        mb_next, m_new = [], []
        for a, b in zip(scores(kb_next, s_next, False), accumulate(kb_cur, s_cur, mb_cur, m)):
            mb_next.append(a)
            m_new.append(b)
        return tuple(mb_next), tuple(m_new)

    block_at = lambda pos: jnp.where(pos == 0, qi, pos - 1)
    m_init = (jnp.full((1, Q_SUB), NEG_INF, F32),) * len(chains)
    mb_a = tuple(scores(qi, sa_ref, True))

    def pair(jj, carry):
        mb_a, m = carry
        p0 = 2 * jj
        mb_b, m = overlapped(block_at(p0 + 1), sb_ref, block_at(p0), sa_ref, mb_a, m)
        mb_a, m = overlapped(block_at(p0 + 2), sa_ref, block_at(p0 + 1), sb_ref, mb_b, m)
        return mb_a, m

    npairs = qi // 2
    mb_a, m = lax.fori_loop(0, npairs, pair, (mb_a, m_init))
    last = 2 * npairs

    @pl.when(last == qi)
    def _():
        tuple(accumulate(block_at(last), sa_ref, mb_a, m))

    @pl.when(last != qi)
    def _():
        mb_b, m2 = overlapped(block_at(last + 1), sb_ref, block_at(last), sa_ref, mb_a, m)
        tuple(accumulate(block_at(last + 1), sb_ref, mb_b, m2))

    sz = sz_ref[...].astype(F32)
    o0 = acc_ref[0, 0:vdim, :] / acc_ref[0, vdim:vdim + 1, :]
    o1 = acc_ref[1, 0:vdim, :] / acc_ref[1, vdim:vdim + 1, :]
    if fox:
        o_t = jnp.concatenate([o0, o1], axis=0)
        o_ref[...] = (o_t.T * sz).astype(BF16)
    else:
        lam = (jnp.exp(jnp.sum(lam_ref[0:1, :] * lam_ref[1:2, :], axis=1, keepdims=True))
               - jnp.exp(jnp.sum(lam_ref[2:3, :] * lam_ref[3:4, :], axis=1, keepdims=True))
               + lam_init)
        o_t = o0 - lam * o1
        var = jnp.mean(o_t * o_t, axis=0, keepdims=True)
        y_t = o_t * lax.rsqrt(var + EPS) * g_ref[...] * (1.0 - lam_init)
        o_ref[...] = (y_t.T * sz).astype(BF16)


def _attention(q, k, vt, sz, *, fox, f=None, lam_vecs=None, subln_g=None, lam_init=0.0, sz_offset=0):
    B, S, _ = q.shape
    T = ATTN_BLOCK
    nq = S // T
    qspec = pl.BlockSpec((None, T, LANES), lambda b, u, i: (b, i, u))
    kspec = pl.BlockSpec((None, S, LANES), lambda b, u, i: (b, 0, u))
    vspec = pl.BlockSpec((None, None, nq, vt.shape[3], T), lambda b, u, i: (b, u, 0, 0, 0))
    szspec = pl.BlockSpec((None, T, LANES), lambda b, u, i: (b, i, u + sz_offset))
    if fox:
        extra = [f]
        extra_specs = [pl.BlockSpec((None, None, S, 2), lambda b, u, i: (b, u, 0, 0))]
        scratch = [pltpu.VMEM((2, FOX_MAP_ROWS, T), F32)]
    else:
        extra = [lam_vecs, subln_g]
        extra_specs = [pl.BlockSpec((4, HEAD_DIM), lambda b, u, i: (0, 0)),
                       pl.BlockSpec((LANES, 1), lambda b, u, i: (0, 0))]
        scratch = [pltpu.VMEM((2, DIFF_MAP_ROWS, T), F32)]
    return pl.pallas_call(
        functools.partial(_attention_kernel, fox=fox, lam_init=lam_init),
        grid=(B, N_UNITS, nq),
        in_specs=[qspec, kspec, vspec] + extra_specs + [szspec],
        out_specs=pl.BlockSpec((None, T, LANES), lambda b, u, i: (b, i, u)),
        out_shape=jax.ShapeDtypeStruct((B, S, N_UNITS * LANES), BF16),
        scratch_shapes=scratch + [pltpu.VMEM((2, T, T), F32)] * 2,
        compiler_params=pltpu.CompilerParams(
            dimension_semantics=("arbitrary", "arbitrary", "arbitrary"),
            vmem_limit_bytes=VMEM_LIMIT),
        name="fox_attention" if fox else "diff_attention",
    )(q, k, vt, *extra, sz)


def _mid_proj_kernel(x_ref, yf_ref, yd_ref, wo_ref, g_ref, w_ref, wg_ref, bg_ref,
                     x1_ref, q_ref, kt_ref, v_ref, gates_ref, gate_ref):
    x1 = (x_ref[...]
          + jnp.dot(yf_ref[...], wo_ref[0:FOX_WIDTH, :], preferred_element_type=F32)
          + jnp.dot(yd_ref[...], wo_ref[FOX_WIDTH:ATTN_WIDTH, :], preferred_element_type=F32))
    x1_ref[...] = x1
    hb = _rms(x1, g_ref[...]).astype(BF16)

    def proj(lo, hi):
        return jnp.dot(hb, w_ref[:, lo:hi], preferred_element_type=F32)

    q_ref[...] = proj(0, 512).astype(BF16)
    kt_ref[...] = proj(512, 1024).T.astype(BF16)
    v_ref[...] = proj(1024, 2048).astype(BF16)
    og = proj(2048, 3072)
    z = proj(3072, 4096)
    gate_ref[...] = (_sigmoid(og) * (z * _sigmoid(z))).astype(BF16)

    gt = jnp.dot(hb, wg_ref[...], preferred_element_type=F32) + bg_ref[...]
    lane = lax.broadcasted_iota(jnp.int32, (1, LANES), 1)
    gates_ref[...] = jnp.where(lane < ML_HEADS, gt, _log_sigmoid(gt))


def _mid_proj(x, yf, yd, wo, g, w, wg, bg):
    B, S, _ = x.shape
    T = ROW_BLOCK
    row = lambda b, i: (b, i, 0)
    const2 = lambda b, i: (0, 0)
    return pl.pallas_call(
        _mid_proj_kernel,
        grid=(B, S // T),
        in_specs=[
            pl.BlockSpec((None, T, D_MODEL), row),
            pl.BlockSpec((None, T, FOX_WIDTH), row),
            pl.BlockSpec((None, T, DIFF_WIDTH), row),
            pl.BlockSpec((ATTN_WIDTH, D_MODEL), const2),
            pl.BlockSpec((1, D_MODEL), const2),
            pl.BlockSpec((D_MODEL, 4096), const2),
            pl.BlockSpec((D_MODEL, LANES), const2),
            pl.BlockSpec((1, LANES), const2),
        ],
        out_specs=(
            pl.BlockSpec((None, T, D_MODEL), row),
            pl.BlockSpec((None, T, ML_QK_WIDTH), row),
            pl.BlockSpec((None, ML_QK_WIDTH, T), lambda b, i: (b, 0, i)),
            pl.BlockSpec((None, T, ML_WIDTH), row),
            pl.BlockSpec((None, T, LANES), row),
            pl.BlockSpec((None, T, ML_WIDTH), row),
        ),
        out_shape=(
            jax.ShapeDtypeStruct((B, S, D_MODEL), F32),
            jax.ShapeDtypeStruct((B, S, ML_QK_WIDTH), BF16),
            jax.ShapeDtypeStruct((B, ML_QK_WIDTH, S), BF16),
            jax.ShapeDtypeStruct((B, S, ML_WIDTH), BF16),
            jax.ShapeDtypeStruct((B, S, LANES), F32),
            jax.ShapeDtypeStruct((B, S, ML_WIDTH), BF16),
        ),
        compiler_params=pltpu.CompilerParams(
            dimension_semantics=("arbitrary", "arbitrary"), vmem_limit_bytes=VMEM_LIMIT),
        name="mid_proj",
    )(x, yf, yd, wo, g, w, wg, bg)


def _mlstm_kernel(q_ref, kt_ref, v_ref, gates_ref, gate_ref, o_ref, c_ref, m_ref):
    L = ML_CHUNK

    @pl.when(pl.program_id(1) == 0)
    def _():
        c_ref[...] = jnp.zeros_like(c_ref)
        m_ref[...] = jnp.zeros_like(m_ref)

    gts = gates_ref[...]
    lane = lax.broadcasted_iota(jnp.int32, (1, LANES), 1)
    cols = jnp.where(lane < ML_HEADS, gts, _tri_cumsum(gts))
    rows = cols.T
    causal = (lax.broadcasted_iota(jnp.int32, (L, L), 0)
              >= lax.broadcasted_iota(jnp.int32, (L, L), 1))
    ones_col = jnp.where(lane == 0, 1.0, 0.0).astype(BF16) + jnp.zeros((L, LANES), BF16)

    for h in range(ML_HEADS):
        li_row = rows[h:h + 1, :]
        b_row = rows[ML_HEADS + h:ML_HEADS + h + 1, :]
        b_col = cols[:, ML_HEADS + h:ML_HEADS + h + 1]
        a = b_col[L - 1:L, :]
        m_prev = m_ref[:, h:h + 1]

        log_d = jnp.where(causal, b_col - b_row + li_row, NEG_INF)
        log_inter = b_col + m_prev
        m_t = jnp.maximum(log_inter, jnp.max(log_d, axis=1, keepdims=True))
        d = jnp.exp(log_d - m_t)
        w_inter = jnp.exp(log_inter - m_t)

        qh = q_ref[:, h * ML_QK_DIM:(h + 1) * ML_QK_DIM]
        kth = kt_ref[h * ML_QK_DIM:(h + 1) * ML_QK_DIM, :]
        vh = v_ref[:, h * ML_V_DIM:(h + 1) * ML_V_DIM]
        vext = jnp.concatenate([vh, ones_col], axis=1)
        cext = c_ref[h]

        s = jnp.dot(qh, kth, preferred_element_type=F32) * d
        sv = jnp.dot(s.astype(BF16), vh, preferred_element_type=F32)
        qc = jnp.dot(qh, cext.astype(BF16), preferred_element_type=F32)
        num = w_inter * qc[:, :ML_V_DIM] + sv
        den = w_inter * qc[:, ML_V_DIM:ML_V_DIM + 1] + jnp.sum(s, axis=1, keepdims=True)
        hh = num / jnp.maximum(jnp.abs(den), jnp.exp(-m_t))

        g_row = a - b_row + li_row
        m_new = jnp.maximum(a + m_prev, jnp.max(g_row, axis=1, keepdims=True))
        w_old = jnp.exp(a + m_prev - m_new)
        kw = (kth.astype(F32) * jnp.exp(g_row - m_new)).astype(BF16)
        c_ref[h] = w_old * cext + jnp.dot(kw, vext, preferred_element_type=F32)
        m_ref[:, h:h + 1] = m_new

        sl = slice(h * ML_V_DIM, (h + 1) * ML_V_DIM)
        o_ref[:, sl] = (hh * gate_ref[:, sl].astype(F32)).astype(BF16)


def _mlstm(q, kt, v, gates, gate):
    B, S, _ = q.shape
    L = ML_CHUNK
    row = lambda b, c: (b, c, 0)
    return pl.pallas_call(
        _mlstm_kernel,
        grid=(B, S // L),
        in_specs=[
            pl.BlockSpec((None, L, ML_QK_WIDTH), row),
            pl.BlockSpec((None, ML_QK_WIDTH, L), lambda b, c: (b, 0, c)),
            pl.BlockSpec((None, L, ML_WIDTH), row),
            pl.BlockSpec((None, L, LANES), row),
            pl.BlockSpec((None, L, ML_WIDTH), row),
        ],
        out_specs=pl.BlockSpec((None, L, ML_WIDTH), row),
        out_shape=jax.ShapeDtypeStruct((B, S, ML_WIDTH), BF16),
        scratch_shapes=[pltpu.VMEM((ML_HEADS, ML_QK_DIM, 2 * ML_V_DIM), F32),
                        pltpu.VMEM((1, LANES), F32)],
        compiler_params=pltpu.CompilerParams(
            dimension_semantics=("arbitrary", "arbitrary"), vmem_limit_bytes=VMEM_LIMIT),
        name="mlstm",
    )(q, kt, v, gates, gate)


def _final_proj_kernel(x_ref, y_ref, wo_ref, g_ref, o_ref):
    x2 = x_ref[...] + jnp.dot(y_ref[...], wo_ref[...], preferred_element_type=F32)
    o_ref[...] = _rms(x2, g_ref[...])


def _final_proj(x1, y, wo, g):
    B, S, _ = x1.shape
    T = ROW_BLOCK
    row = lambda b, i: (b, i, 0)
    const2 = lambda b, i: (0, 0)
    return pl.pallas_call(
        _final_proj_kernel,
        grid=(B, S // T),
        in_specs=[
            pl.BlockSpec((None, T, D_MODEL), row),
            pl.BlockSpec((None, T, ML_WIDTH), row),
            pl.BlockSpec((ML_WIDTH, D_MODEL), const2),
            pl.BlockSpec((1, D_MODEL), const2),
        ],
        out_specs=pl.BlockSpec((None, T, D_MODEL), row),
        out_shape=jax.ShapeDtypeStruct((B, S, D_MODEL), F32),
        compiler_params=pltpu.CompilerParams(
            dimension_semantics=("arbitrary", "arbitrary"), vmem_limit_bytes=VMEM_LIMIT),
        name="final_proj",
    )(x1, y, wo, g)


def _rope_lane_perm():
    n = np.arange(LANES)
    half, c, r = n // 64, (n % 64) // 32, n % 32
    per_head = c * HEAD_DIM + half * 32 + r
    return np.concatenate([h * LANES + per_head for h in range(DIFF_HEADS)])


def _pad_lanes(t):
    return jnp.pad(t, ((0, 0), (0, LANES - t.shape[1])))


def kernel(x, positions, attn_norm_g, attn_w_in, fox_b_f, diff_lam_q1, diff_lam_k1, diff_lam_q2,
           diff_lam_k2, diff_subln_g, attn_w_out, ml_norm_g, ml_w_in, ml_b_i, ml_b_f, ml_w_out,
           final_norm_g):
    B, S, _ = x.shape
    scale = HEAD_DIM ** -0.5 * LOG2E

    w = attn_w_in[0]
    o = np.cumsum((0, FOX_WIDTH, FOX_WIDTH, FOX_WIDTH, FOX_HEADS, DIFF_WIDTH, DIFF_WIDTH, DIFF_WIDTH,
                   ATTN_WIDTH))
    perm = _rope_lane_perm()
    w_main = jnp.concatenate([
        w[:, o[0]:o[1]] * scale, w[:, o[1]:o[2]], w[:, o[2]:o[3]],
        w[:, o[4]:o[5]][:, perm] * scale, w[:, o[5]:o[6]][:, perm], w[:, o[6]:o[7]],
        w[:, o[7]:o[8]]], axis=1).astype(BF16)
    w_f = _pad_lanes(w[:, o[3]:o[4]]).astype(BF16)
    b_f = _pad_lanes(fox_b_f[0][None, :])
    half = HEAD_DIM // 2
    inv = ROPE_THETA ** (-jnp.arange(half, dtype=F32) / half)
    inv = jnp.tile(inv, LANES // half)[None, :]
    posf = positions.astype(F32)[..., None]

    qf, kf, vft, fcum, qd, kd, vdt, sz = _attn_proj(
        x, posf, attn_norm_g[0][None, :], w_main, w_f, b_f, inv)

    f_units = fcum.reshape(B, S, N_UNITS, 2).transpose(0, 2, 1, 3)
    y_fox = _attention(qf, kf, vft, sz, fox=True, f=f_units, sz_offset=0)
    lam_vecs = jnp.stack([diff_lam_q1[0], diff_lam_k1[0], diff_lam_q2[0], diff_lam_k2[0]])
    lam_init = 0.8 - 0.6 * math.exp(-0.3 * 0)
    y_diff = _attention(qd, kd, vdt, sz, fox=False, lam_vecs=lam_vecs,
                        subln_g=diff_subln_g[0][:, None], lam_init=lam_init, sz_offset=N_UNITS)

    w2 = ml_w_in[0]
    o2 = np.cumsum((0, ML_QK_WIDTH, ML_QK_WIDTH, ML_WIDTH, ML_HEADS, ML_HEADS, ML_WIDTH, ML_WIDTH))
    w2_main = jnp.concatenate([
        w2[:, o2[0]:o2[1]], w2[:, o2[1]:o2[2]] * (ML_QK_DIM ** -0.5), w2[:, o2[2]:o2[3]],
        w2[:, o2[5]:o2[6]], w2[:, o2[6]:o2[7]]], axis=1).astype(BF16)
    w2_g = _pad_lanes(w2[:, o2[3]:o2[5]]).astype(BF16)
    b_g = _pad_lanes(jnp.concatenate([ml_b_i[0], ml_b_f[0]])[None, :])

    x1, q2, k2t, v2, gates, gate = _mid_proj(
        x, y_fox, y_diff, attn_w_out[0].astype(BF16), ml_norm_g[0][None, :], w2_main, w2_g, b_g)
    y2 = _mlstm(q2, k2t, v2, gates, gate)
    return _final_proj(x1, y2, ml_w_out[0].astype(BF16), final_norm_g[None, :])
```

```python
import functools
import math

import numpy as np
import jax
import jax.numpy as jnp
from jax import lax
from jax.experimental import pallas as pl
from jax.experimental.pallas import tpu as pltpu

F32 = jnp.float32
BF16 = jnp.bfloat16

D_MODEL = 1024
HEAD_DIM = 64
FOX_HEADS = 8
DIFF_HEADS = 4
FOX_WIDTH = FOX_HEADS * HEAD_DIM
DIFF_WIDTH = DIFF_HEADS * 2 * HEAD_DIM
ATTN_WIDTH = FOX_WIDTH + DIFF_WIDTH
ML_HEADS = 8
ML_QK_DIM = 64
ML_V_DIM = 128
ML_QK_WIDTH = ML_HEADS * ML_QK_DIM
ML_WIDTH = ML_HEADS * ML_V_DIM
ROPE_THETA = 10000.0
EPS = 1e-6

LANES = 128
N_UNITS = 4
ATTN_BLOCK = 512
Q_BLOCK = 1024
Q_SUB = 256
ONES_ROWS = 16
FOX_MAP_ROWS = HEAD_DIM + ONES_ROWS
DIFF_MAP_ROWS = 2 * HEAD_DIM + ONES_ROWS
LOG2E = math.log2(math.e)
ML_CHUNK = 128
ROW_BLOCK = 512
VMEM_LIMIT = 56 * 1024 * 1024

NEG_INF = float("-inf")


def _split3(x):
    hi = x.astype(BF16)
    r1 = x - hi.astype(F32)
    mid = r1.astype(BF16)
    lo = (r1 - mid.astype(F32)).astype(BF16)
    return hi, mid, lo


def _tri_cumsum(x):
    n = x.shape[0]
    row = lax.broadcasted_iota(jnp.int32, (n, n), 0)
    col = lax.broadcasted_iota(jnp.int32, (n, n), 1)
    tri = jnp.where(row >= col, 1.0, 0.0).astype(BF16)
    hi, mid, lo = _split3(x)
    return (jnp.dot(tri, hi, preferred_element_type=F32)
            + jnp.dot(tri, mid, preferred_element_type=F32)
            + jnp.dot(tri, lo, preferred_element_type=F32))


def _log_sigmoid(x):
    return jnp.minimum(x, 0.0) - jnp.log(1.0 + jnp.exp(-jnp.abs(x)))


def _sigmoid(x):
    return 1.0 / (1.0 + jnp.exp(-x))


def _rms(x, g):
    return x * lax.rsqrt(jnp.mean(x * x, axis=-1, keepdims=True) + EPS) * g


def _attn_proj_kernel(x_ref, pos_ref, g_ref, w_ref, wf_ref, bf_ref, inv_ref,
                      qf_ref, kf_ref, vft_ref, f_ref, qd_ref, kd_ref, vdt_ref, sz_ref,
                      carry_ref):
    @pl.when(pl.program_id(1) == 0)
    def _():
        carry_ref[...] = jnp.zeros_like(carry_ref)

    hb = _rms(x_ref[...], g_ref[...]).astype(BF16)

    def proj(lo, hi):
        return jnp.dot(hb, w_ref[:, lo:hi], preferred_element_type=F32)

    qf_ref[...] = proj(0, 512).astype(BF16)
    kf_ref[...] = proj(512, 1024).astype(BF16)
    ones = jnp.ones((ONES_ROWS, ROW_BLOCK), BF16)
    vf = proj(1024, 1536)
    for u in range(N_UNITS):
        t = vf[:, u * LANES:(u + 1) * LANES].T.astype(BF16)
        for c in range(2):
            base = c * FOX_MAP_ROWS
            vft_ref[u, base:base + HEAD_DIM, :] = t[c * HEAD_DIM:(c + 1) * HEAD_DIM, :]
            vft_ref[u, base + HEAD_DIM:base + FOX_MAP_ROWS, :] = ones

    ang = pos_ref[...] * inv_ref[...]
    cos = jnp.cos(ang)
    sin = jnp.sin(ang)
    lane = lax.broadcasted_iota(jnp.int32, (1, LANES), 1)
    sin_signed = jnp.where(lane < 64, -sin, sin)

    def rope_store(lo, out_ref):
        t = proj(lo, lo + 512)
        for h in range(DIFF_HEADS):
            xh = t[:, h * LANES:(h + 1) * LANES]
            out_ref[:, h * LANES:(h + 1) * LANES] = (
                xh * cos + pltpu.roll(xh, 64, 1) * sin_signed).astype(BF16)

    rope_store(1536, qd_ref)
    rope_store(2048, kd_ref)
    vd = proj(2560, 3072)
    for u in range(N_UNITS):
        vdt_ref[u, 0:LANES, :] = vd[:, u * LANES:(u + 1) * LANES].T.astype(BF16)
        vdt_ref[u, LANES:DIFF_MAP_ROWS, :] = ones

    z = proj(3072, 4096)
    sz_ref[...] = (z * _sigmoid(z)).astype(BF16)

    lf = _log_sigmoid(jnp.dot(hb, wf_ref[...], preferred_element_type=F32) + bf_ref[...])
    cs = _tri_cumsum(lf) + carry_ref[...]
    carry_ref[...] = cs[ROW_BLOCK - 1:ROW_BLOCK, :]
    f_ref[...] = cs[:, :FOX_HEADS] * LOG2E


def _attn_proj(x, posf, g, w, wf, bf, inv):
    B, S, _ = x.shape
    T = ROW_BLOCK
    nb = S // T
    row = lambda b, i: (b, i, 0)
    const2 = lambda b, i: (0, 0)
    vt_spec = lambda rows: pl.BlockSpec((None, N_UNITS, None, rows, T), lambda b, i: (b, 0, i, 0, 0))
    out_shape = (
        jax.ShapeDtypeStruct((B, S, 512), BF16),
        jax.ShapeDtypeStruct((B, S, 512), BF16),
        jax.ShapeDtypeStruct((B, N_UNITS, nb, 2 * FOX_MAP_ROWS, T), BF16),
        jax.ShapeDtypeStruct((B, S, FOX_HEADS), F32),
        jax.ShapeDtypeStruct((B, S, 512), BF16),
        jax.ShapeDtypeStruct((B, S, 512), BF16),
        jax.ShapeDtypeStruct((B, N_UNITS, nb, DIFF_MAP_ROWS, T), BF16),
        jax.ShapeDtypeStruct((B, S, ATTN_WIDTH), BF16),
    )
    return pl.pallas_call(
        _attn_proj_kernel,
        grid=(B, nb),
        in_specs=[
            pl.BlockSpec((None, T, D_MODEL), row),
            pl.BlockSpec((None, T, 1), row),
            pl.BlockSpec((1, D_MODEL), const2),
            pl.BlockSpec((D_MODEL, 4096), const2),
            pl.BlockSpec((D_MODEL, LANES), const2),
            pl.BlockSpec((1, LANES), const2),
            pl.BlockSpec((1, LANES), const2),
        ],
        out_specs=(
            pl.BlockSpec((None, T, 512), row),
            pl.BlockSpec((None, T, 512), row),
            vt_spec(2 * FOX_MAP_ROWS),
            pl.BlockSpec((None, T, FOX_HEADS), row),
            pl.BlockSpec((None, T, 512), row),
            pl.BlockSpec((None, T, 512), row),
            vt_spec(DIFF_MAP_ROWS),
            pl.BlockSpec((None, T, ATTN_WIDTH), row),
        ),
        out_shape=out_shape,
        scratch_shapes=[pltpu.VMEM((1, LANES), F32)],
        compiler_params=pltpu.CompilerParams(
            dimension_semantics=("arbitrary", "arbitrary"), vmem_limit_bytes=VMEM_LIMIT),
        name="attn_proj",
    )(x, posf, g, w, wf, bf, inv)


def _attention_kernel(*refs, fox, lam_init):
    if fox:
        q_ref, k_ref, vt_ref, f_ref, sz_ref, o_ref, acc_ref, sa_ref, sb_ref = refs
    else:
        q_ref, k_ref, vt_ref, lam_ref, g_ref, sz_ref, o_ref, acc_ref, sa_ref, sb_ref = refs
    T = ATTN_BLOCK
    n_sub = Q_BLOCK // Q_SUB
    vrows = FOX_MAP_ROWS if fox else DIFF_MAP_ROWS
    vdim = vrows - ONES_ROWS
    qi = pl.program_id(2)
    kb_diag = (Q_BLOCK // T) * qi

    q = q_ref[...]
    lane = lax.broadcasted_iota(jnp.int32, (1, LANES), 1)
    first = (lane < 64) if fox else ((lane % 64) < 32)
    zero = jnp.zeros_like(q)
    qm = (jnp.where(first, q, zero), jnp.where(first, zero, q))
    qm = [[qc[j * Q_SUB:(j + 1) * Q_SUB, :] for j in range(n_sub)] for qc in qm]

    acc_ref[...] = jnp.zeros_like(acc_ref)
    if fox:
        q0 = pl.multiple_of(qi * Q_BLOCK, Q_BLOCK)
        f0 = [f_ref[pl.ds(q0, 1), c:c + 1] for c in range(2)]

    chains = [(c, j) for c in range(2) for j in range(n_sub)]
    half = n_sub // 2

    def scores(kb, s_ref, diag=None):
        start = pl.multiple_of(kb * T, T)
        k = k_ref[pl.ds(start, T), :]
        if fox:
            bias = [f_ref[pl.ds(start, T), c:c + 1] - f0[c] for c in range(2)]
        if diag is not None:
            kpos = lax.broadcasted_iota(jnp.int32, (T, Q_SUB), 0)
            qpos = lax.broadcasted_iota(jnp.int32, (T, Q_SUB), 1)
        for c, j in chains:
            lanes = slice(j * Q_SUB, (j + 1) * Q_SUB)
            if diag == 1 and j < half:
                s_ref[c, :, lanes] = jnp.full((T, Q_SUB), NEG_INF, F32)
                yield jnp.full((1, Q_SUB), NEG_INF, F32)
                continue
            sc = lax.dot_general(k, qm[c][j], (((1,), (1,)), ((), ())),
                                 preferred_element_type=F32)
            if fox:
                sc = sc - bias[c]
            if diag is not None and j // half == diag:
                sc = jnp.where(kpos <= qpos + (j % half) * Q_SUB, sc, NEG_INF)
            s_ref[c, :, lanes] = sc
            yield jnp.max(sc, axis=0, keepdims=True)

    def accumulate(kb, s_ref, mb, m):
        for i, (c, j) in enumerate(chains):
            lanes = slice(j * Q_SUB, (j + 1) * Q_SUB)
            m_new = jnp.maximum(m[i], mb[i])
            alpha = jnp.exp2(m[i] - m_new)
            pb = jnp.exp2(s_ref[c, :, lanes] - m_new).astype(BF16)
            vt = vt_ref[kb, 0:vrows, :] if not fox else vt_ref[kb, c * vrows:(c + 1) * vrows, :]
            acc_ref[c, :, lanes] = alpha * acc_ref[c, :, lanes] + jnp.dot(
                vt, pb, preferred_element_type=F32)
            yield m_new

    def overlapped(next_scores, kb_cur, s_cur, mb_cur, m):
        mb_next, m_new = [], []
        for a, b in zip(next_scores, accumulate(kb_cur, s_cur, mb_cur, m)):
            mb_next.append(a)
            m_new.append(b)
        return tuple(mb_next), tuple(m_new)

    m_init = (jnp.full((1, Q_SUB), NEG_INF, F32),) * len(chains)
    mb_a = tuple(scores(kb_diag, sa_ref, diag=0))
    mb_b, m = overlapped(scores(kb_diag + 1, sb_ref, diag=1), kb_diag, sa_ref, mb_a, m_init)

    def pair(jj, carry):
        mb_b, m = carry
        kb_b = jnp.where(jj == 0, kb_diag + 1, 2 * jj - 1)
        mb_a, m = overlapped(scores(2 * jj, sa_ref), kb_b, sb_ref, mb_b, m)
        mb_b, m = overlapped(scores(2 * jj + 1, sb_ref), 2 * jj, sa_ref, mb_a, m)
        return mb_b, m

    mb_b, m = lax.fori_loop(0, qi, pair, (mb_b, m))
    tuple(accumulate(jnp.where(qi == 0, kb_diag + 1, kb_diag - 1), sb_ref, mb_b, m))

    sz = sz_ref[...].astype(F32)
    o0 = acc_ref[0, 0:vdim, :] / acc_ref[0, vdim:vdim + 1, :]
    o1 = acc_ref[1, 0:vdim, :] / acc_ref[1, vdim:vdim + 1, :]
    if fox:
        o_t = jnp.concatenate([o0, o1], axis=0)
        o_ref[...] = (o_t.T * sz).astype(BF16)
    else:
        lam = (jnp.exp(jnp.sum(lam_ref[0:1, :] * lam_ref[1:2, :], axis=1, keepdims=True))
               - jnp.exp(jnp.sum(lam_ref[2:3, :] * lam_ref[3:4, :], axis=1, keepdims=True))
               + lam_init)
        o_t = o0 - lam * o1
        var = jnp.mean(o_t * o_t, axis=0, keepdims=True)
        y_t = o_t * lax.rsqrt(var + EPS) * g_ref[...] * (1.0 - lam_init)
        o_ref[...] = (y_t.T * sz).astype(BF16)


def _attention(q, k, vt, sz, *, fox, f=None, lam_vecs=None, subln_g=None, lam_init=0.0, sz_offset=0):
    B, S, _ = q.shape
    T = ATTN_BLOCK
    nk = S // T
    nq = S // Q_BLOCK
    qspec = pl.BlockSpec((None, Q_BLOCK, LANES), lambda b, u, i: (b, i, u))
    kspec = pl.BlockSpec((None, S, LANES), lambda b, u, i: (b, 0, u))
    vspec = pl.BlockSpec((None, None, nk, vt.shape[3], T), lambda b, u, i: (b, u, 0, 0, 0))
    szspec = pl.BlockSpec((None, Q_BLOCK, LANES), lambda b, u, i: (b, i, u + sz_offset))
    if fox:
        extra = [f]
        extra_specs = [pl.BlockSpec((None, None, S, 2), lambda b, u, i: (b, u, 0, 0))]
        scratch = [pltpu.VMEM((2, FOX_MAP_ROWS, Q_BLOCK), F32)]
    else:
        extra = [lam_vecs, subln_g]
        extra_specs = [pl.BlockSpec((4, HEAD_DIM), lambda b, u, i: (0, 0)),
                       pl.BlockSpec((LANES, 1), lambda b, u, i: (0, 0))]
        scratch = [pltpu.VMEM((2, DIFF_MAP_ROWS, Q_BLOCK), F32)]
    return pl.pallas_call(
        functools.partial(_attention_kernel, fox=fox, lam_init=lam_init),
        grid=(B, N_UNITS, nq),
        in_specs=[qspec, kspec, vspec] + extra_specs + [szspec],
        out_specs=pl.BlockSpec((None, Q_BLOCK, LANES), lambda b, u, i: (b, i, u)),
        out_shape=jax.ShapeDtypeStruct((B, S, N_UNITS * LANES), BF16),
        scratch_shapes=scratch + [pltpu.VMEM((2, T, Q_BLOCK), F32)] * 2,
        compiler_params=pltpu.CompilerParams(
            dimension_semantics=("arbitrary", "arbitrary", "arbitrary"),
            vmem_limit_bytes=VMEM_LIMIT),
        name="fox_attention" if fox else "diff_attention",
    )(q, k, vt, *extra, sz)


def _mid_proj_kernel(x_ref, yf_ref, yd_ref, wo_ref, g_ref, w_ref, wg_ref, bg_ref,
                     x1_ref, q_ref, kt_ref, v_ref, gates_ref, gate_ref):
    x1 = (x_ref[...]
          + jnp.dot(yf_ref[...], wo_ref[0:FOX_WIDTH, :], preferred_element_type=F32)
          + jnp.dot(yd_ref[...], wo_ref[FOX_WIDTH:ATTN_WIDTH, :], preferred_element_type=F32))
    x1_ref[...] = x1
    hb = _rms(x1, g_ref[...]).astype(BF16)

    def proj(lo, hi):
        return jnp.dot(hb, w_ref[:, lo:hi], preferred_element_type=F32)

    q_ref[...] = proj(0, 512).astype(BF16)
    kt_ref[...] = proj(512, 1024).T.astype(BF16)
    v_ref[...] = proj(1024, 2048).astype(BF16)
    og = proj(2048, 3072)
    z = proj(3072, 4096)
    gate_ref[...] = (_sigmoid(og) * (z * _sigmoid(z))).astype(BF16)

    gt = jnp.dot(hb, wg_ref[...], preferred_element_type=F32) + bg_ref[...]
    lane = lax.broadcasted_iota(jnp.int32, (1, LANES), 1)
    gates_ref[...] = jnp.where(lane < ML_HEADS, gt, _log_sigmoid(gt))


def _mid_proj(x, yf, yd, wo, g, w, wg, bg):
    B, S, _ = x.shape
    T = ROW_BLOCK
    row = lambda b, i: (b, i, 0)
    const2 = lambda b, i: (0, 0)
    return pl.pallas_call(
        _mid_proj_kernel,
        grid=(B, S // T),
        in_specs=[
            pl.BlockSpec((None, T, D_MODEL), row),
            pl.BlockSpec((None, T, FOX_WIDTH), row),
            pl.BlockSpec((None, T, DIFF_WIDTH), row),
            pl.BlockSpec((ATTN_WIDTH, D_MODEL), const2),
            pl.BlockSpec((1, D_MODEL), const2),
            pl.BlockSpec((D_MODEL, 4096), const2),
            pl.BlockSpec((D_MODEL, LANES), const2),
            pl.BlockSpec((1, LANES), const2),
        ],
        out_specs=(
            pl.BlockSpec((None, T, D_MODEL), row),
            pl.BlockSpec((None, T, ML_QK_WIDTH), row),
            pl.BlockSpec((None, ML_QK_WIDTH, T), lambda b, i: (b, 0, i)),
            pl.BlockSpec((None, T, ML_WIDTH), row),
            pl.BlockSpec((None, T, LANES), row),
            pl.BlockSpec((None, T, ML_WIDTH), row),
        ),
        out_shape=(
            jax.ShapeDtypeStruct((B, S, D_MODEL), F32),
            jax.ShapeDtypeStruct((B, S, ML_QK_WIDTH), BF16),
            jax.ShapeDtypeStruct((B, ML_QK_WIDTH, S), BF16),
            jax.ShapeDtypeStruct((B, S, ML_WIDTH), BF16),
            jax.ShapeDtypeStruct((B, S, LANES), F32),
            jax.ShapeDtypeStruct((B, S, ML_WIDTH), BF16),
        ),
        compiler_params=pltpu.CompilerParams(
            dimension_semantics=("arbitrary", "arbitrary"), vmem_limit_bytes=VMEM_LIMIT),
        name="mid_proj",
    )(x, yf, yd, wo, g, w, wg, bg)


def _mlstm_kernel(q_ref, kt_ref, v_ref, gates_ref, gate_ref, o_ref, c_ref, m_ref):
    L = ML_CHUNK

    @pl.when(pl.program_id(1) == 0)
    def _():
        c_ref[...] = jnp.zeros_like(c_ref)
        m_ref[...] = jnp.zeros_like(m_ref)

    gts = gates_ref[...]
    lane = lax.broadcasted_iota(jnp.int32, (1, LANES), 1)
    cols = jnp.where(lane < ML_HEADS, gts, _tri_cumsum(gts))
    rows = cols.T
    causal = (lax.broadcasted_iota(jnp.int32, (L, L), 0)
              >= lax.broadcasted_iota(jnp.int32, (L, L), 1))
    ones_col = jnp.where(lane == 0, 1.0, 0.0).astype(BF16) + jnp.zeros((L, LANES), BF16)

    for h in range(ML_HEADS):
        li_row = rows[h:h + 1, :]
        b_row = rows[ML_HEADS + h:ML_HEADS + h + 1, :]
        b_col = cols[:, ML_HEADS + h:ML_HEADS + h + 1]
        a = b_col[L - 1:L, :]
        m_prev = m_ref[:, h:h + 1]

        log_d = jnp.where(causal, b_col - b_row + li_row, NEG_INF)
        log_inter = b_col + m_prev
        m_t = jnp.maximum(log_inter, jnp.max(log_d, axis=1, keepdims=True))
        d = jnp.exp(log_d - m_t)
        w_inter = jnp.exp(log_inter - m_t)

        qh = q_ref[:, h * ML_QK_DIM:(h + 1) * ML_QK_DIM]
        kth = kt_ref[h * ML_QK_DIM:(h + 1) * ML_QK_DIM, :]
        vh = v_ref[:, h * ML_V_DIM:(h + 1) * ML_V_DIM]
        vext = jnp.concatenate([vh, ones_col], axis=1)
        cext = c_ref[h]

        s = jnp.dot(qh, kth, preferred_element_type=F32) * d
        sv = jnp.dot(s.astype(BF16), vh, preferred_element_type=F32)
        qc = jnp.dot(qh, cext.astype(BF16), preferred_element_type=F32)
        num = w_inter * qc[:, :ML_V_DIM] + sv
        den = w_inter * qc[:, ML_V_DIM:ML_V_DIM + 1] + jnp.sum(s, axis=1, keepdims=True)
        hh = num / jnp.maximum(jnp.abs(den), jnp.exp(-m_t))

        g_row = a - b_row + li_row
        m_new = jnp.maximum(a + m_prev, jnp.max(g_row, axis=1, keepdims=True))
        w_old = jnp.exp(a + m_prev - m_new)
        kw = (kth.astype(F32) * jnp.exp(g_row - m_new)).astype(BF16)
        c_ref[h] = w_old * cext + jnp.dot(kw, vext, preferred_element_type=F32)
        m_ref[:, h:h + 1] = m_new

        sl = slice(h * ML_V_DIM, (h + 1) * ML_V_DIM)
        o_ref[:, sl] = (hh * gate_ref[:, sl].astype(F32)).astype(BF16)


def _mlstm(q, kt, v, gates, gate):
    B, S, _ = q.shape
    L = ML_CHUNK
    row = lambda b, c: (b, c, 0)
    return pl.pallas_call(
        _mlstm_kernel,
        grid=(B, S // L),
        in_specs=[
            pl.BlockSpec((None, L, ML_QK_WIDTH), row),
            pl.BlockSpec((None, ML_QK_WIDTH, L), lambda b, c: (b, 0, c)),
            pl.BlockSpec((None, L, ML_WIDTH), row),
            pl.BlockSpec((None, L, LANES), row),
            pl.BlockSpec((None, L, ML_WIDTH), row),
        ],
        out_specs=pl.BlockSpec((None, L, ML_WIDTH), row),
        out_shape=jax.ShapeDtypeStruct((B, S, ML_WIDTH), BF16),
        scratch_shapes=[pltpu.VMEM((ML_HEADS, ML_QK_DIM, 2 * ML_V_DIM), F32),
                        pltpu.VMEM((1, LANES), F32)],
        compiler_params=pltpu.CompilerParams(
            dimension_semantics=("arbitrary", "arbitrary"), vmem_limit_bytes=VMEM_LIMIT),
        name="mlstm",
    )(q, kt, v, gates, gate)


def _final_proj_kernel(x_ref, y_ref, wo_ref, g_ref, o_ref):
    x2 = x_ref[...] + jnp.dot(y_ref[...], wo_ref[...], preferred_element_type=F32)
    o_ref[...] = _rms(x2, g_ref[...])


def _final_proj(x1, y, wo, g):
    B, S, _ = x1.shape
    T = ROW_BLOCK
    row = lambda b, i: (b, i, 0)
    const2 = lambda b, i: (0, 0)
    return pl.pallas_call(
        _final_proj_kernel,
        grid=(B, S // T),
        in_specs=[
            pl.BlockSpec((None, T, D_MODEL), row),
            pl.BlockSpec((None, T, ML_WIDTH), row),
            pl.BlockSpec((ML_WIDTH, D_MODEL), const2),
            pl.BlockSpec((1, D_MODEL), const2),
        ],
        out_specs=pl.BlockSpec((None, T, D_MODEL), row),
        out_shape=jax.ShapeDtypeStruct((B, S, D_MODEL), F32),
        compiler_params=pltpu.CompilerParams(
            dimension_semantics=("arbitrary", "arbitrary"), vmem_limit_bytes=VMEM_LIMIT),
        name="final_proj",
    )(x1, y, wo, g)


def _rope_lane_perm():
    n = np.arange(LANES)
    half, c, r = n // 64, (n % 64) // 32, n % 32
    per_head = c * HEAD_DIM + half * 32 + r
    return np.concatenate([h * LANES + per_head for h in range(DIFF_HEADS)])


def _pad_lanes(t):
    return jnp.pad(t, ((0, 0), (0, LANES - t.shape[1])))


def kernel(x, positions, attn_norm_g, attn_w_in, fox_b_f, diff_lam_q1, diff_lam_k1, diff_lam_q2,
           diff_lam_k2, diff_subln_g, attn_w_out, ml_norm_g, ml_w_in, ml_b_i, ml_b_f, ml_w_out,
           final_norm_g):
    B, S, _ = x.shape
    scale = HEAD_DIM ** -0.5 * LOG2E

    w = attn_w_in[0]
    o = np.cumsum((0, FOX_WIDTH, FOX_WIDTH, FOX_WIDTH, FOX_HEADS, DIFF_WIDTH, DIFF_WIDTH, DIFF_WIDTH,
                   ATTN_WIDTH))
    perm = _rope_lane_perm()
    w_main = jnp.concatenate([
        w[:, o[0]:o[1]] * scale, w[:, o[1]:o[2]], w[:, o[2]:o[3]],
        w[:, o[4]:o[5]][:, perm] * scale, w[:, o[5]:o[6]][:, perm], w[:, o[6]:o[7]],
        w[:, o[7]:o[8]]], axis=1).astype(BF16)
    w_f = _pad_lanes(w[:, o[3]:o[4]]).astype(BF16)
    b_f = _pad_lanes(fox_b_f[0][None, :])
    half = HEAD_DIM // 2
    inv = ROPE_THETA ** (-jnp.arange(half, dtype=F32) / half)
    inv = jnp.tile(inv, LANES // half)[None, :]
    posf = positions.astype(F32)[..., None]

    qf, kf, vft, fcum, qd, kd, vdt, sz = _attn_proj(
        x, posf, attn_norm_g[0][None, :], w_main, w_f, b_f, inv)

    f_units = fcum.reshape(B, S, N_UNITS, 2).transpose(0, 2, 1, 3)
    y_fox = _attention(qf, kf, vft, sz, fox=True, f=f_units, sz_offset=0)
    lam_vecs = jnp.stack([diff_lam_q1[0], diff_lam_k1[0], diff_lam_q2[0], diff_lam_k2[0]])
    lam_init = 0.8 - 0.6 * math.exp(-0.3 * 0)
    y_diff = _attention(qd, kd, vdt, sz, fox=False, lam_vecs=lam_vecs,
                        subln_g=diff_subln_g[0][:, None], lam_init=lam_init, sz_offset=N_UNITS)

    w2 = ml_w_in[0]
    o2 = np.cumsum((0, ML_QK_WIDTH, ML_QK_WIDTH, ML_WIDTH, ML_HEADS, ML_HEADS, ML_WIDTH, ML_WIDTH))
    w2_main = jnp.concatenate([
        w2[:, o2[0]:o2[1]], w2[:, o2[1]:o2[2]] * (ML_QK_DIM ** -0.5), w2[:, o2[2]:o2[3]],
        w2[:, o2[5]:o2[6]], w2[:, o2[6]:o2[7]]], axis=1).astype(BF16)
    w2_g = _pad_lanes(w2[:, o2[3]:o2[5]]).astype(BF16)
    b_g = _pad_lanes(jnp.concatenate([ml_b_i[0], ml_b_f[0]])[None, :])

    x1, q2, k2t, v2, gates, gate = _mid_proj(
        x, y_fox, y_diff, attn_w_out[0].astype(BF16), ml_norm_g[0][None, :], w2_main, w2_g, b_g)
    y2 = _mlstm(q2, k2t, v2, gates, gate)
    return _final_proj(x1, y2, ml_w_out[0].astype(BF16), final_norm_g[None, :])
```

```python
import functools
import math

import numpy as np
import jax
import jax.numpy as jnp
from jax import lax
from jax.experimental import pallas as pl
from jax.experimental.pallas import tpu as pltpu

F32 = jnp.float32
BF16 = jnp.bfloat16

D_MODEL = 1024
HEAD_DIM = 64
FOX_HEADS = 8
DIFF_HEADS = 4
FOX_WIDTH = FOX_HEADS * HEAD_DIM
DIFF_WIDTH = DIFF_HEADS * 2 * HEAD_DIM
ATTN_WIDTH = FOX_WIDTH + DIFF_WIDTH
ML_HEADS = 8
ML_QK_DIM = 64
ML_V_DIM = 128
ML_QK_WIDTH = ML_HEADS * ML_QK_DIM
ML_WIDTH = ML_HEADS * ML_V_DIM
ROPE_THETA = 10000.0
EPS = 1e-6

LANES = 128
N_UNITS = 4
ATTN_BLOCK = 512
Q_BLOCK = 1024
Q_SUB = 256
ONES_ROWS = 16
FOX_MAP_ROWS = HEAD_DIM + ONES_ROWS
DIFF_MAP_ROWS = 2 * HEAD_DIM + ONES_ROWS
LOG2E = math.log2(math.e)
ML_CHUNK = 256
ML_VT_ROWS = ML_V_DIM + ONES_ROWS
ROW_BLOCK = 512
VMEM_LIMIT = 56 * 1024 * 1024

NEG_INF = float("-inf")


def _split3(x):
    hi = x.astype(BF16)
    r1 = x - hi.astype(F32)
    mid = r1.astype(BF16)
    lo = (r1 - mid.astype(F32)).astype(BF16)
    return hi, mid, lo


def _tri_cumsum(x):
    n = x.shape[0]
    row = lax.broadcasted_iota(jnp.int32, (n, n), 0)
    col = lax.broadcasted_iota(jnp.int32, (n, n), 1)
    tri = jnp.where(row >= col, 1.0, 0.0).astype(BF16)
    hi, mid, lo = _split3(x)
    return (jnp.dot(tri, hi, preferred_element_type=F32)
            + jnp.dot(tri, mid, preferred_element_type=F32)
            + jnp.dot(tri, lo, preferred_element_type=F32))


def _log_sigmoid(x):
    return jnp.minimum(x, 0.0) - jnp.log(1.0 + jnp.exp(-jnp.abs(x)))


def _sigmoid(x):
    return 1.0 / (1.0 + jnp.exp(-x))


def _rms(x, g):
    return x * lax.rsqrt(jnp.mean(x * x, axis=-1, keepdims=True) + EPS) * g


def _attn_proj_kernel(x_ref, pos_ref, g_ref, w_ref, wf_ref, bf_ref, inv_ref,
                      qf_ref, kf_ref, vft_ref, f_ref, qd_ref, kd_ref, vdt_ref, sz_ref,
                      carry_ref):
    @pl.when(pl.program_id(1) == 0)
    def _():
        carry_ref[...] = jnp.zeros_like(carry_ref)

    hb = _rms(x_ref[...], g_ref[...]).astype(BF16)

    def proj(lo, hi):
        return jnp.dot(hb, w_ref[:, lo:hi], preferred_element_type=F32)

    qf_ref[...] = proj(0, 512).astype(BF16)
    kf_ref[...] = proj(512, 1024).astype(BF16)
    ones = jnp.ones((ONES_ROWS, ROW_BLOCK), BF16)
    vf = proj(1024, 1536)
    for u in range(N_UNITS):
        t = vf[:, u * LANES:(u + 1) * LANES].T.astype(BF16)
        for c in range(2):
            base = c * FOX_MAP_ROWS
            vft_ref[u, base:base + HEAD_DIM, :] = t[c * HEAD_DIM:(c + 1) * HEAD_DIM, :]
            vft_ref[u, base + HEAD_DIM:base + FOX_MAP_ROWS, :] = ones

    ang = pos_ref[...] * inv_ref[...]
    cos = jnp.cos(ang)
    sin = jnp.sin(ang)
    lane = lax.broadcasted_iota(jnp.int32, (1, LANES), 1)
    sin_signed = jnp.where(lane < 64, -sin, sin)

    def rope_store(lo, out_ref):
        t = proj(lo, lo + 512)
        for h in range(DIFF_HEADS):
            xh = t[:, h * LANES:(h + 1) * LANES]
            out_ref[:, h * LANES:(h + 1) * LANES] = (
                xh * cos + pltpu.roll(xh, 64, 1) * sin_signed).astype(BF16)

    rope_store(1536, qd_ref)
    rope_store(2048, kd_ref)
    vd = proj(2560, 3072)
    for u in range(N_UNITS):
        vdt_ref[u, 0:LANES, :] = vd[:, u * LANES:(u + 1) * LANES].T.astype(BF16)
        vdt_ref[u, LANES:DIFF_MAP_ROWS, :] = ones

    z = proj(3072, 4096)
    sz_ref[...] = (z * _sigmoid(z)).astype(BF16)

    lf = _log_sigmoid(jnp.dot(hb, wf_ref[...], preferred_element_type=F32) + bf_ref[...])
    cs = _tri_cumsum(lf) + carry_ref[...]
    carry_ref[...] = cs[ROW_BLOCK - 1:ROW_BLOCK, :]
    f_ref[...] = cs[:, :FOX_HEADS] * LOG2E


def _attn_proj(x, posf, g, w, wf, bf, inv):
    B, S, _ = x.shape
    T = ROW_BLOCK
    nb = S // T
    row = lambda b, i: (b, i, 0)
    const2 = lambda b, i: (0, 0)
    vt_spec = lambda rows: pl.BlockSpec((None, N_UNITS, None, rows, T), lambda b, i: (b, 0, i, 0, 0))
    out_shape = (
        jax.ShapeDtypeStruct((B, S, 512), BF16),
        jax.ShapeDtypeStruct((B, S, 512), BF16),
        jax.ShapeDtypeStruct((B, N_UNITS, nb, 2 * FOX_MAP_ROWS, T), BF16),
        jax.ShapeDtypeStruct((B, S, FOX_HEADS), F32),
        jax.ShapeDtypeStruct((B, S, 512), BF16),
        jax.ShapeDtypeStruct((B, S, 512), BF16),
        jax.ShapeDtypeStruct((B, N_UNITS, nb, DIFF_MAP_ROWS, T), BF16),
        jax.ShapeDtypeStruct((B, S, ATTN_WIDTH), BF16),
    )
    return pl.pallas_call(
        _attn_proj_kernel,
        grid=(B, nb),
        in_specs=[
            pl.BlockSpec((None, T, D_MODEL), row),
            pl.BlockSpec((None, T, 1), row),
            pl.BlockSpec((1, D_MODEL), const2),
            pl.BlockSpec((D_MODEL, 4096), const2),
            pl.BlockSpec((D_MODEL, LANES), const2),
            pl.BlockSpec((1, LANES), const2),
            pl.BlockSpec((1, LANES), const2),
        ],
        out_specs=(
            pl.BlockSpec((None, T, 512), row),
            pl.BlockSpec((None, T, 512), row),
            vt_spec(2 * FOX_MAP_ROWS),
            pl.BlockSpec((None, T, FOX_HEADS), row),
            pl.BlockSpec((None, T, 512), row),
            pl.BlockSpec((None, T, 512), row),
            vt_spec(DIFF_MAP_ROWS),
            pl.BlockSpec((None, T, ATTN_WIDTH), row),
        ),
        out_shape=out_shape,
        scratch_shapes=[pltpu.VMEM((1, LANES), F32)],
        compiler_params=pltpu.CompilerParams(
            dimension_semantics=("arbitrary", "arbitrary"), vmem_limit_bytes=VMEM_LIMIT),
        name="attn_proj",
    )(x, posf, g, w, wf, bf, inv)


def _attention_kernel(*refs, fox, lam_init):
    if fox:
        q_ref, k_ref, vt_ref, f_ref, sz_ref, o_ref, acc_ref, sa_ref, sb_ref = refs
    else:
        q_ref, k_ref, vt_ref, lam_ref, g_ref, sz_ref, o_ref, acc_ref, sa_ref, sb_ref = refs
    T = ATTN_BLOCK
    n_sub = Q_BLOCK // Q_SUB
    vrows = FOX_MAP_ROWS if fox else DIFF_MAP_ROWS
    vdim = vrows - ONES_ROWS
    qi = pl.program_id(2)
    kb_diag = (Q_BLOCK // T) * qi

    q = q_ref[...]
    lane = lax.broadcasted_iota(jnp.int32, (1, LANES), 1)
    first = (lane < 64) if fox else ((lane % 64) < 32)
    zero = jnp.zeros_like(q)
    qm = (jnp.where(first, q, zero), jnp.where(first, zero, q))
    qm = [[qc[j * Q_SUB:(j + 1) * Q_SUB, :] for j in range(n_sub)] for qc in qm]

    acc_ref[...] = jnp.zeros_like(acc_ref)
    if fox:
        q0 = pl.multiple_of(qi * Q_BLOCK, Q_BLOCK)
        f0 = [f_ref[pl.ds(q0, 1), c:c + 1] for c in range(2)]

    chains = [(c, j) for c in range(2) for j in range(n_sub)]
    half = n_sub // 2

    def scores(kb, s_ref, diag=None):
        start = pl.multiple_of(kb * T, T)
        k = k_ref[pl.ds(start, T), :]
        if fox:
            bias = [f_ref[pl.ds(start, T), c:c + 1] - f0[c] for c in range(2)]
        if diag is not None:
            kpos = lax.broadcasted_iota(jnp.int32, (T, Q_SUB), 0)
            qpos = lax.broadcasted_iota(jnp.int32, (T, Q_SUB), 1)
        for c, j in chains:
            lanes = slice(j * Q_SUB, (j + 1) * Q_SUB)
            if diag == 1 and j < half:
                s_ref[c, :, lanes] = jnp.full((T, Q_SUB), NEG_INF, F32)
                yield jnp.full((1, Q_SUB), NEG_INF, F32)
                continue
            sc = lax.dot_general(k, qm[c][j], (((1,), (1,)), ((), ())),
                                 preferred_element_type=F32)
            if fox:
                sc = sc - bias[c]
            if diag is not None and j // half == diag:
                sc = jnp.where(kpos <= qpos + (j % half) * Q_SUB, sc, NEG_INF)
            s_ref[c, :, lanes] = sc
            yield jnp.max(sc, axis=0, keepdims=True)

    def accumulate(kb, s_ref, mb, m):
        for i, (c, j) in enumerate(chains):
            lanes = slice(j * Q_SUB, (j + 1) * Q_SUB)
            m_new = jnp.maximum(m[i], mb[i])
            alpha = jnp.exp2(m[i] - m_new)
            pb = jnp.exp2(s_ref[c, :, lanes] - m_new).astype(BF16)
            vt = vt_ref[kb, 0:vrows, :] if not fox else vt_ref[kb, c * vrows:(c + 1) * vrows, :]
            acc_ref[c, :, lanes] = alpha * acc_ref[c, :, lanes] + jnp.dot(
                vt, pb, preferred_element_type=F32)
            yield m_new

    def overlapped(next_scores, kb_cur, s_cur, mb_cur, m):
        mb_next, m_new = [], []
        for a, b in zip(next_scores, accumulate(kb_cur, s_cur, mb_cur, m)):
            mb_next.append(a)
            m_new.append(b)
        return tuple(mb_next), tuple(m_new)

    m_init = (jnp.full((1, Q_SUB), NEG_INF, F32),) * len(chains)
    mb_a = tuple(scores(kb_diag, sa_ref, diag=0))
    mb_b, m = overlapped(scores(kb_diag + 1, sb_ref, diag=1), kb_diag, sa_ref, mb_a, m_init)

    def pair(jj, carry):
        mb_b, m = carry
        kb_b = jnp.where(jj == 0, kb_diag + 1, 2 * jj - 1)
        mb_a, m = overlapped(scores(2 * jj, sa_ref), kb_b, sb_ref, mb_b, m)
        mb_b, m = overlapped(scores(2 * jj + 1, sb_ref), 2 * jj, sa_ref, mb_a, m)
        return mb_b, m

    mb_b, m = lax.fori_loop(0, qi, pair, (mb_b, m))
    tuple(accumulate(jnp.where(qi == 0, kb_diag + 1, kb_diag - 1), sb_ref, mb_b, m))

    sz = sz_ref[...].astype(F32)
    o0 = acc_ref[0, 0:vdim, :] / acc_ref[0, vdim:vdim + 1, :]
    o1 = acc_ref[1, 0:vdim, :] / acc_ref[1, vdim:vdim + 1, :]
    if fox:
        o_t = jnp.concatenate([o0, o1], axis=0)
        o_ref[...] = (o_t.T * sz).astype(BF16)
    else:
        lam = (jnp.exp(jnp.sum(lam_ref[0:1, :] * lam_ref[1:2, :], axis=1, keepdims=True))
               - jnp.exp(jnp.sum(lam_ref[2:3, :] * lam_ref[3:4, :], axis=1, keepdims=True))
               + lam_init)
        o_t = o0 - lam * o1
        var = jnp.mean(o_t * o_t, axis=0, keepdims=True)
        y_t = o_t * lax.rsqrt(var + EPS) * g_ref[...] * (1.0 - lam_init)
        o_ref[...] = (y_t.T * sz).astype(BF16)


def _attention(q, k, vt, sz, *, fox, f=None, lam_vecs=None, subln_g=None, lam_init=0.0, sz_offset=0):
    B, S, _ = q.shape
    T = ATTN_BLOCK
    nk = S // T
    nq = S // Q_BLOCK
    qspec = pl.BlockSpec((None, Q_BLOCK, LANES), lambda b, u, i: (b, i, u))
    kspec = pl.BlockSpec((None, S, LANES), lambda b, u, i: (b, 0, u))
    vspec = pl.BlockSpec((None, None, nk, vt.shape[3], T), lambda b, u, i: (b, u, 0, 0, 0))
    szspec = pl.BlockSpec((None, Q_BLOCK, LANES), lambda b, u, i: (b, i, u + sz_offset))
    if fox:
        extra = [f]
        extra_specs = [pl.BlockSpec((None, None, S, 2), lambda b, u, i: (b, u, 0, 0))]
        scratch = [pltpu.VMEM((2, FOX_MAP_ROWS, Q_BLOCK), F32)]
    else:
        extra = [lam_vecs, subln_g]
        extra_specs = [pl.BlockSpec((4, HEAD_DIM), lambda b, u, i: (0, 0)),
                       pl.BlockSpec((LANES, 1), lambda b, u, i: (0, 0))]
        scratch = [pltpu.VMEM((2, DIFF_MAP_ROWS, Q_BLOCK), F32)]
    return pl.pallas_call(
        functools.partial(_attention_kernel, fox=fox, lam_init=lam_init),
        grid=(B, N_UNITS, nq),
        in_specs=[qspec, kspec, vspec] + extra_specs + [szspec],
        out_specs=pl.BlockSpec((None, Q_BLOCK, LANES), lambda b, u, i: (b, i, u)),
        out_shape=jax.ShapeDtypeStruct((B, S, N_UNITS * LANES), BF16),
        scratch_shapes=scratch + [pltpu.VMEM((2, T, Q_BLOCK), F32)] * 2,
        compiler_params=pltpu.CompilerParams(
            dimension_semantics=("arbitrary", "arbitrary", "arbitrary"),
            vmem_limit_bytes=VMEM_LIMIT),
        name="fox_attention" if fox else "diff_attention",
    )(q, k, vt, *extra, sz)


def _mid_proj_kernel(x_ref, yf_ref, yd_ref, wo_ref, g_ref, w_ref, wg_ref, bg_ref,
                     x1_ref, q_ref, k_ref, vt_ref, gates_ref, gate_ref):
    x1 = (x_ref[...]
          + jnp.dot(yf_ref[...], wo_ref[0:FOX_WIDTH, :], preferred_element_type=F32)
          + jnp.dot(yd_ref[...], wo_ref[FOX_WIDTH:ATTN_WIDTH, :], preferred_element_type=F32))
    x1_ref[...] = x1
    hb = _rms(x1, g_ref[...]).astype(BF16)

    def proj(lo, hi):
        return jnp.dot(hb, w_ref[:, lo:hi], preferred_element_type=F32)

    q_ref[...] = proj(0, 512).astype(BF16)
    k_ref[...] = proj(512, 1024).astype(BF16)
    v = proj(1024, 2048)
    ones = jnp.ones((ONES_ROWS, ROW_BLOCK), BF16)
    for h in range(ML_HEADS):
        base = h * ML_VT_ROWS
        vt_ref[base:base + ML_V_DIM, :] = v[:, h * ML_V_DIM:(h + 1) * ML_V_DIM].T.astype(BF16)
        vt_ref[base + ML_V_DIM:base + ML_VT_ROWS, :] = ones
    og = proj(2048, 3072)
    z = proj(3072, 4096)
    gate_ref[...] = (_sigmoid(og) * (z * _sigmoid(z))).astype(BF16)

    gt = jnp.dot(hb, wg_ref[...], preferred_element_type=F32) + bg_ref[...]
    lane = lax.broadcasted_iota(jnp.int32, (1, LANES), 1)
    gates_ref[...] = jnp.where(lane < ML_HEADS, gt, _log_sigmoid(gt))


def _mid_proj(x, yf, yd, wo, g, w, wg, bg):
    B, S, _ = x.shape
    T = ROW_BLOCK
    row = lambda b, i: (b, i, 0)
    const2 = lambda b, i: (0, 0)
    return pl.pallas_call(
        _mid_proj_kernel,
        grid=(B, S // T),
        in_specs=[
            pl.BlockSpec((None, T, D_MODEL), row),
            pl.BlockSpec((None, T, FOX_WIDTH), row),
            pl.BlockSpec((None, T, DIFF_WIDTH), row),
            pl.BlockSpec((ATTN_WIDTH, D_MODEL), const2),
            pl.BlockSpec((1, D_MODEL), const2),
            pl.BlockSpec((D_MODEL, 4096), const2),
            pl.BlockSpec((D_MODEL, LANES), const2),
            pl.BlockSpec((1, LANES), const2),
        ],
        out_specs=(
            pl.BlockSpec((None, T, D_MODEL), row),
            pl.BlockSpec((None, T, ML_QK_WIDTH), row),
            pl.BlockSpec((None, T, ML_QK_WIDTH), row),
            pl.BlockSpec((None, ML_HEADS * ML_VT_ROWS, T), lambda b, i: (b, 0, i)),
            pl.BlockSpec((None, T, LANES), row),
            pl.BlockSpec((None, T, ML_WIDTH), row),
        ),
        out_shape=(
            jax.ShapeDtypeStruct((B, S, D_MODEL), F32),
            jax.ShapeDtypeStruct((B, S, ML_QK_WIDTH), BF16),
            jax.ShapeDtypeStruct((B, S, ML_QK_WIDTH), BF16),
            jax.ShapeDtypeStruct((B, ML_HEADS * ML_VT_ROWS, S), BF16),
            jax.ShapeDtypeStruct((B, S, LANES), F32),
            jax.ShapeDtypeStruct((B, S, ML_WIDTH), BF16),
        ),
        compiler_params=pltpu.CompilerParams(
            dimension_semantics=("arbitrary", "arbitrary"), vmem_limit_bytes=VMEM_LIMIT),
        name="mid_proj",
    )(x, yf, yd, wo, g, w, wg, bg)


def _mlstm_kernel(q_ref, k_ref, vt_ref, gates_ref, gate_ref, o_ref, ct_ref, m_ref):
    L = ML_CHUNK
    NT = (((1,), (1,)), ((), ()))

    @pl.when(pl.program_id(1) == 0)
    def _():
        ct_ref[...] = jnp.zeros_like(ct_ref)
        m_ref[...] = jnp.zeros_like(m_ref)

    gts = gates_ref[...]
    lane = lax.broadcasted_iota(jnp.int32, (1, LANES), 1)
    cols = jnp.where(lane < ML_HEADS, gts, _tri_cumsum(gts))
    rows = cols.T
    allowed = (lax.broadcasted_iota(jnp.int32, (L, L), 0)
               <= lax.broadcasted_iota(jnp.int32, (L, L), 1))

    def decay_and_scores(h):
        li_row = rows[h:h + 1, :]
        b_row = rows[ML_HEADS + h:ML_HEADS + h + 1, :]
        u_col = cols[:, h:h + 1] - cols[:, ML_HEADS + h:ML_HEADS + h + 1]
        a = b_row[:, L - 1:L]
        m_prev = m_ref[:, h:h + 1]
        log_d = jnp.where(allowed, u_col + b_row, NEG_INF)
        log_inter = b_row + m_prev
        m_t = jnp.maximum(log_inter, jnp.max(log_d, axis=0, keepdims=True))
        d = jnp.exp(log_d - m_t)
        w_inter = jnp.exp(log_inter - m_t)
        qh = q_ref[:, h * ML_QK_DIM:(h + 1) * ML_QK_DIM]
        kh = k_ref[:, h * ML_QK_DIM:(h + 1) * ML_QK_DIM]
        ct = ct_ref[h]
        kq = lax.dot_general(kh, qh, NT, preferred_element_type=F32)
        ctq = lax.dot_general(ct.astype(BF16), qh, NT, preferred_element_type=F32)
        return dict(li_row=li_row, b_row=b_row, a=a, m_prev=m_prev, m_t=m_t, d=d,
                    w_inter=w_inter, kh=kh, ct=ct, kq=kq, ctq=ctq)

    def outputs_and_state(h, st):
        vt = vt_ref[h * ML_VT_ROWS:(h + 1) * ML_VT_ROWS, :]
        s = (st["kq"] * st["d"]).astype(BF16)
        tot = st["w_inter"] * st["ctq"] + jnp.dot(vt, s, preferred_element_type=F32)
        den = tot[ML_V_DIM:ML_V_DIM + 1, :]
        h_t = tot[0:ML_V_DIM, :] / jnp.maximum(jnp.abs(den), jnp.exp(-st["m_t"]))
        sl = slice(h * ML_V_DIM, (h + 1) * ML_V_DIM)
        o_ref[:, sl] = (h_t.T * gate_ref[:, sl].astype(F32)).astype(BF16)

        g_row = st["a"] - st["b_row"] + st["li_row"]
        m_new = jnp.maximum(st["a"] + st["m_prev"], jnp.max(g_row, axis=1, keepdims=True))
        w_old = jnp.exp(st["a"] + st["m_prev"] - m_new)
        vw = (vt.astype(F32) * jnp.exp(g_row - m_new)).astype(BF16)
        ct_ref[h] = w_old * st["ct"] + jnp.dot(vw, st["kh"], preferred_element_type=F32)
        m_ref[:, h:h + 1] = m_new

    ahead = decay_and_scores(0)
    for h in range(ML_HEADS):
        cur = ahead
        if h + 1 < ML_HEADS:
            ahead = decay_and_scores(h + 1)
        outputs_and_state(h, cur)


def _mlstm(q, k, vt, gates, gate):
    B, S, _ = q.shape
    L = ML_CHUNK
    row = lambda b, c: (b, c, 0)
    return pl.pallas_call(
        _mlstm_kernel,
        grid=(B, S // L),
        in_specs=[
            pl.BlockSpec((None, L, ML_QK_WIDTH), row),
            pl.BlockSpec((None, L, ML_QK_WIDTH), row),
            pl.BlockSpec((None, ML_HEADS * ML_VT_ROWS, L), lambda b, c: (b, 0, c)),
            pl.BlockSpec((None, L, LANES), row),
            pl.BlockSpec((None, L, ML_WIDTH), row),
        ],
        out_specs=pl.BlockSpec((None, L, ML_WIDTH), row),
        out_shape=jax.ShapeDtypeStruct((B, S, ML_WIDTH), BF16),
        scratch_shapes=[pltpu.VMEM((ML_HEADS, ML_VT_ROWS, ML_QK_DIM), F32),
                        pltpu.VMEM((1, LANES), F32)],
        compiler_params=pltpu.CompilerParams(
            dimension_semantics=("arbitrary", "arbitrary"), vmem_limit_bytes=VMEM_LIMIT),
        name="mlstm",
    )(q, k, vt, gates, gate)


def _final_proj_kernel(x_ref, y_ref, wo_ref, g_ref, o_ref):
    x2 = x_ref[...] + jnp.dot(y_ref[...], wo_ref[...], preferred_element_type=F32)
    o_ref[...] = _rms(x2, g_ref[...])


def _final_proj(x1, y, wo, g):
    B, S, _ = x1.shape
    T = ROW_BLOCK
    row = lambda b, i: (b, i, 0)
    const2 = lambda b, i: (0, 0)
    return pl.pallas_call(
        _final_proj_kernel,
        grid=(B, S // T),
        in_specs=[
            pl.BlockSpec((None, T, D_MODEL), row),
            pl.BlockSpec((None, T, ML_WIDTH), row),
            pl.BlockSpec((ML_WIDTH, D_MODEL), const2),
            pl.BlockSpec((1, D_MODEL), const2),
        ],
        out_specs=pl.BlockSpec((None, T, D_MODEL), row),
        out_shape=jax.ShapeDtypeStruct((B, S, D_MODEL), F32),
        compiler_params=pltpu.CompilerParams(
            dimension_semantics=("arbitrary", "arbitrary"), vmem_limit_bytes=VMEM_LIMIT),
        name="final_proj",
    )(x1, y, wo, g)


def _rope_lane_perm():
    n = np.arange(LANES)
    half, c, r = n // 64, (n % 64) // 32, n % 32
    per_head = c * HEAD_DIM + half * 32 + r
    return np.concatenate([h * LANES + per_head for h in range(DIFF_HEADS)])


def _pad_lanes(t):
    return jnp.pad(t, ((0, 0), (0, LANES - t.shape[1])))


def kernel(x, positions, attn_norm_g, attn_w_in, fox_b_f, diff_lam_q1, diff_lam_k1, diff_lam_q2,
           diff_lam_k2, diff_subln_g, attn_w_out, ml_norm_g, ml_w_in, ml_b_i, ml_b_f, ml_w_out,
           final_norm_g):
    B, S, _ = x.shape
    scale = HEAD_DIM ** -0.5 * LOG2E

    w = attn_w_in[0]
    o = np.cumsum((0, FOX_WIDTH, FOX_WIDTH, FOX_WIDTH, FOX_HEADS, DIFF_WIDTH, DIFF_WIDTH, DIFF_WIDTH,
                   ATTN_WIDTH))
    perm = _rope_lane_perm()
    w_main = jnp.concatenate([
        w[:, o[0]:o[1]] * scale, w[:, o[1]:o[2]], w[:, o[2]:o[3]],
        w[:, o[4]:o[5]][:, perm] * scale, w[:, o[5]:o[6]][:, perm], w[:, o[6]:o[7]],
        w[:, o[7]:o[8]]], axis=1).astype(BF16)
    w_f = _pad_lanes(w[:, o[3]:o[4]]).astype(BF16)
    b_f = _pad_lanes(fox_b_f[0][None, :])
    half = HEAD_DIM // 2
    inv = ROPE_THETA ** (-jnp.arange(half, dtype=F32) / half)
    inv = jnp.tile(inv, LANES // half)[None, :]
    posf = positions.astype(F32)[..., None]

    qf, kf, vft, fcum, qd, kd, vdt, sz = _attn_proj(
        x, posf, attn_norm_g[0][None, :], w_main, w_f, b_f, inv)

    f_units = fcum.reshape(B, S, N_UNITS, 2).transpose(0, 2, 1, 3)
    y_fox = _attention(qf, kf, vft, sz, fox=True, f=f_units, sz_offset=0)
    lam_vecs = jnp.stack([diff_lam_q1[0], diff_lam_k1[0], diff_lam_q2[0], diff_lam_k2[0]])
    lam_init = 0.8 - 0.6 * math.exp(-0.3 * 0)
    y_diff = _attention(qd, kd, vdt, sz, fox=False, lam_vecs=lam_vecs,
                        subln_g=diff_subln_g[0][:, None], lam_init=lam_init, sz_offset=N_UNITS)

    w2 = ml_w_in[0]
    o2 = np.cumsum((0, ML_QK_WIDTH, ML_QK_WIDTH, ML_WIDTH, ML_HEADS, ML_HEADS, ML_WIDTH, ML_WIDTH))
    w2_main = jnp.concatenate([
        w2[:, o2[0]:o2[1]], w2[:, o2[1]:o2[2]] * (ML_QK_DIM ** -0.5), w2[:, o2[2]:o2[3]],
        w2[:, o2[5]:o2[6]], w2[:, o2[6]:o2[7]]], axis=1).astype(BF16)
    w2_g = _pad_lanes(w2[:, o2[3]:o2[5]]).astype(BF16)
    b_g = _pad_lanes(jnp.concatenate([ml_b_i[0], ml_b_f[0]])[None, :])

    x1, q2, k2, v2t, gates, gate = _mid_proj(
        x, y_fox, y_diff, attn_w_out[0].astype(BF16), ml_norm_g[0][None, :], w2_main, w2_g, b_g)
    y2 = _mlstm(q2, k2, v2t, gates, gate)
    return _final_proj(x1, y2, ml_w_out[0].astype(BF16), final_norm_g[None, :])
```

```python
import functools
import math

import numpy as np
import jax
import jax.numpy as jnp
from jax import lax
from jax.experimental import pallas as pl
from jax.experimental.pallas import tpu as pltpu

F32 = jnp.float32
BF16 = jnp.bfloat16

D_MODEL = 1024
HEAD_DIM = 64
FOX_HEADS = 8
DIFF_HEADS = 4
FOX_WIDTH = FOX_HEADS * HEAD_DIM
DIFF_WIDTH = DIFF_HEADS * 2 * HEAD_DIM
ATTN_WIDTH = FOX_WIDTH + DIFF_WIDTH
ML_HEADS = 8
ML_QK_DIM = 64
ML_V_DIM = 128
ML_QK_WIDTH = ML_HEADS * ML_QK_DIM
ML_WIDTH = ML_HEADS * ML_V_DIM
ROPE_THETA = 10000.0
EPS = 1e-6

LANES = 128
N_UNITS = 4
ATTN_BLOCK = 512
Q_BLOCK = 2048
Q_SUB = 256
ONES_ROWS = 16
FOX_MAP_ROWS = HEAD_DIM + ONES_ROWS
DIFF_MAP_ROWS = 2 * HEAD_DIM + ONES_ROWS
LOG2E = math.log2(math.e)
ML_CHUNK = 256
ML_VT_ROWS = ML_V_DIM + ONES_ROWS
ROW_BLOCK = 512
VMEM_LIMIT = 56 * 1024 * 1024

NEG_INF = float("-inf")


def _split3(x):
    hi = x.astype(BF16)
    r1 = x - hi.astype(F32)
    mid = r1.astype(BF16)
    lo = (r1 - mid.astype(F32)).astype(BF16)
    return hi, mid, lo


def _tri_cumsum(x):
    n = x.shape[0]
    row = lax.broadcasted_iota(jnp.int32, (n, n), 0)
    col = lax.broadcasted_iota(jnp.int32, (n, n), 1)
    tri = jnp.where(row >= col, 1.0, 0.0).astype(BF16)
    hi, mid, lo = _split3(x)
    return (jnp.dot(tri, hi, preferred_element_type=F32)
            + jnp.dot(tri, mid, preferred_element_type=F32)
            + jnp.dot(tri, lo, preferred_element_type=F32))


def _log_sigmoid(x):
    return jnp.minimum(x, 0.0) - jnp.log(1.0 + jnp.exp(-jnp.abs(x)))


def _sigmoid(x):
    return 1.0 / (1.0 + jnp.exp(-x))


def _rms(x, g):
    return x * lax.rsqrt(jnp.mean(x * x, axis=-1, keepdims=True) + EPS) * g


def _attn_proj_kernel(x_ref, pos_ref, g_ref, w_ref, wf_ref, bf_ref, inv_ref,
                      qf_ref, kf_ref, vft_ref, f_ref, qd_ref, kd_ref, vdt_ref, sz_ref,
                      carry_ref):
    @pl.when(pl.program_id(1) == 0)
    def _():
        carry_ref[...] = jnp.zeros_like(carry_ref)

    hb = _rms(x_ref[...], g_ref[...]).astype(BF16)

    def proj(lo, hi):
        return jnp.dot(hb, w_ref[:, lo:hi], preferred_element_type=F32)

    qf_ref[...] = proj(0, 512).astype(BF16)
    kf_ref[...] = proj(512, 1024).astype(BF16)
    ones = jnp.ones((ONES_ROWS, ROW_BLOCK), BF16)
    vf = proj(1024, 1536)
    for u in range(N_UNITS):
        t = vf[:, u * LANES:(u + 1) * LANES].T.astype(BF16)
        for c in range(2):
            base = c * FOX_MAP_ROWS
            vft_ref[u, base:base + HEAD_DIM, :] = t[c * HEAD_DIM:(c + 1) * HEAD_DIM, :]
            vft_ref[u, base + HEAD_DIM:base + FOX_MAP_ROWS, :] = ones

    ang = pos_ref[...] * inv_ref[...]
    cos = jnp.cos(ang)
    sin = jnp.sin(ang)
    lane = lax.broadcasted_iota(jnp.int32, (1, LANES), 1)
    sin_signed = jnp.where(lane < 64, -sin, sin)

    def rope_store(lo, out_ref):
        t = proj(lo, lo + 512)
        for h in range(DIFF_HEADS):
            xh = t[:, h * LANES:(h + 1) * LANES]
            out_ref[:, h * LANES:(h + 1) * LANES] = (
                xh * cos + pltpu.roll(xh, 64, 1) * sin_signed).astype(BF16)

    rope_store(1536, qd_ref)
    rope_store(2048, kd_ref)
    vd = proj(2560, 3072)
    for u in range(N_UNITS):
        vdt_ref[u, 0:LANES, :] = vd[:, u * LANES:(u + 1) * LANES].T.astype(BF16)
        vdt_ref[u, LANES:DIFF_MAP_ROWS, :] = ones

    z = proj(3072, 4096)
    sz_ref[...] = (z * _sigmoid(z)).astype(BF16)

    lf = _log_sigmoid(jnp.dot(hb, wf_ref[...], preferred_element_type=F32) + bf_ref[...])
    cs = _tri_cumsum(lf) + carry_ref[...]
    carry_ref[...] = cs[ROW_BLOCK - 1:ROW_BLOCK, :]
    for u in range(N_UNITS):
        f_ref[u] = cs[:, 2 * u:2 * u + 2] * LOG2E


def _attn_proj(x, posf, g, w, wf, bf, inv):
    B, S, _ = x.shape
    T = ROW_BLOCK
    nb = S // T
    row = lambda b, i: (b, i, 0)
    const2 = lambda b, i: (0, 0)
    vt_spec = lambda rows: pl.BlockSpec((None, N_UNITS, None, rows, T), lambda b, i: (b, 0, i, 0, 0))
    out_shape = (
        jax.ShapeDtypeStruct((B, S, 512), BF16),
        jax.ShapeDtypeStruct((B, S, 512), BF16),
        jax.ShapeDtypeStruct((B, N_UNITS, nb, 2 * FOX_MAP_ROWS, T), BF16),
        jax.ShapeDtypeStruct((B, N_UNITS, S, 2), F32),
        jax.ShapeDtypeStruct((B, S, 512), BF16),
        jax.ShapeDtypeStruct((B, S, 512), BF16),
        jax.ShapeDtypeStruct((B, N_UNITS, nb, DIFF_MAP_ROWS, T), BF16),
        jax.ShapeDtypeStruct((B, S, ATTN_WIDTH), BF16),
    )
    return pl.pallas_call(
        _attn_proj_kernel,
        grid=(B, nb),
        in_specs=[
            pl.BlockSpec((None, T, D_MODEL), row),
            pl.BlockSpec((None, T, 1), row),
            pl.BlockSpec((1, D_MODEL), const2),
            pl.BlockSpec((D_MODEL, 4096), const2),
            pl.BlockSpec((D_MODEL, LANES), const2),
            pl.BlockSpec((1, LANES), const2),
            pl.BlockSpec((1, LANES), const2),
        ],
        out_specs=(
            pl.BlockSpec((None, T, 512), row),
            pl.BlockSpec((None, T, 512), row),
            vt_spec(2 * FOX_MAP_ROWS),
            pl.BlockSpec((None, N_UNITS, T, 2), lambda b, i: (b, 0, i, 0)),
            pl.BlockSpec((None, T, 512), row),
            pl.BlockSpec((None, T, 512), row),
            vt_spec(DIFF_MAP_ROWS),
            pl.BlockSpec((None, T, ATTN_WIDTH), row),
        ),
        out_shape=out_shape,
        scratch_shapes=[pltpu.VMEM((1, LANES), F32)],
        compiler_params=pltpu.CompilerParams(
            dimension_semantics=("arbitrary", "arbitrary"), vmem_limit_bytes=VMEM_LIMIT),
        name="attn_proj",
    )(x, posf, g, w, wf, bf, inv)


def _attention_kernel(*refs, fox, lam_init):
    if fox:
        q_ref, k_ref, vt_ref, f_ref, sz_ref, o_ref, acc_ref, sa_ref, sb_ref = refs
    else:
        q_ref, k_ref, vt_ref, lam_ref, g_ref, sz_ref, o_ref, acc_ref, sa_ref, sb_ref = refs
    T = ATTN_BLOCK
    n_sub = Q_BLOCK // Q_SUB
    vrows = FOX_MAP_ROWS if fox else DIFF_MAP_ROWS
    vdim = vrows - ONES_ROWS
    qi = pl.program_id(2)
    kb_diag = (Q_BLOCK // T) * qi

    q = q_ref[...]
    lane = lax.broadcasted_iota(jnp.int32, (1, LANES), 1)
    first = (lane < 64) if fox else ((lane % 64) < 32)
    zero = jnp.zeros_like(q)
    qm = (jnp.where(first, q, zero), jnp.where(first, zero, q))
    qm = [[qc[j * Q_SUB:(j + 1) * Q_SUB, :] for j in range(n_sub)] for qc in qm]

    acc_ref[...] = jnp.zeros_like(acc_ref)
    if fox:
        q0 = pl.multiple_of(qi * Q_BLOCK, Q_BLOCK)
        f0 = [f_ref[pl.ds(q0, 1), c:c + 1] for c in range(2)]

    chains = [(c, j) for c in range(2) for j in range(n_sub)]
    n_diag = Q_BLOCK // T
    per_blk = n_sub // n_diag

    def scores(kb, s_ref, diag=None):
        start = pl.multiple_of(kb * T, T)
        k = k_ref[pl.ds(start, T), :]
        if fox:
            bias = [f_ref[pl.ds(start, T), c:c + 1] - f0[c] for c in range(2)]
        if diag is not None:
            kpos = lax.broadcasted_iota(jnp.int32, (T, Q_SUB), 0)
            qpos = lax.broadcasted_iota(jnp.int32, (T, Q_SUB), 1)
        for c, j in chains:
            if diag is not None and j // per_blk < diag:
                yield None
                continue
            sc = lax.dot_general(k, qm[c][j], (((1,), (1,)), ((), ())),
                                 preferred_element_type=F32)
            if fox:
                sc = sc - bias[c]
            if diag is not None and j // per_blk == diag:
                sc = jnp.where(kpos <= qpos + (j % per_blk) * Q_SUB, sc, NEG_INF)
            s_ref[c, :, j * Q_SUB:(j + 1) * Q_SUB] = sc
            yield jnp.max(sc, axis=0, keepdims=True)

    def accumulate(kb, s_ref, mb, m):
        for i, (c, j) in enumerate(chains):
            if mb[i] is None:
                yield m[i]
                continue
            lanes = slice(j * Q_SUB, (j + 1) * Q_SUB)
            m_new = jnp.maximum(m[i], mb[i])
            alpha = jnp.exp2(m[i] - m_new)
            pb = jnp.exp2(s_ref[c, :, lanes] - m_new).astype(BF16)
            vt = vt_ref[kb, 0:vrows, :] if not fox else vt_ref[kb, c * vrows:(c + 1) * vrows, :]
            acc_ref[c, :, lanes] = alpha * acc_ref[c, :, lanes] + jnp.dot(
                vt, pb, preferred_element_type=F32)
            yield m_new

    def overlapped(next_scores, kb_cur, s_cur, mb_cur, m):
        mb_next, m_new = [], []
        for a, b in zip(next_scores, accumulate(kb_cur, s_cur, mb_cur, m)):
            mb_next.append(a)
            m_new.append(b)
        return tuple(mb_next), tuple(m_new)

    bufs = (sa_ref, sb_ref)
    m = (jnp.full((1, Q_SUB), NEG_INF, F32),) * len(chains)
    mb = tuple(scores(kb_diag + n_diag - 1, bufs[0], diag=n_diag - 1))
    for i in range(1, n_diag):
        d = n_diag - 1 - i
        mb, m = overlapped(scores(kb_diag + d, bufs[i % 2], diag=d),
                           kb_diag + d + 1, bufs[(i - 1) % 2], mb, m)

    def pair(jj, carry):
        mb_b, m = carry
        kb_b = jnp.where(jj == 0, kb_diag, 2 * jj - 1)
        mb_a, m = overlapped(scores(2 * jj, sa_ref), kb_b, sb_ref, mb_b, m)
        mb_b, m = overlapped(scores(2 * jj + 1, sb_ref), 2 * jj, sa_ref, mb_a, m)
        return mb_b, m

    mb, m = lax.fori_loop(0, kb_diag // 2, pair, (mb, m))
    tuple(accumulate(jnp.where(qi == 0, kb_diag, kb_diag - 1), sb_ref, mb, m))

    sz = sz_ref[...].astype(F32)
    o0 = acc_ref[0, 0:vdim, :] / acc_ref[0, vdim:vdim + 1, :]
    o1 = acc_ref[1, 0:vdim, :] / acc_ref[1, vdim:vdim + 1, :]
    if fox:
        o_t = jnp.concatenate([o0, o1], axis=0)
        o_ref[...] = (o_t.T * sz).astype(BF16)
    else:
        lam = (jnp.exp(jnp.sum(lam_ref[0:1, :] * lam_ref[1:2, :], axis=1, keepdims=True))
               - jnp.exp(jnp.sum(lam_ref[2:3, :] * lam_ref[3:4, :], axis=1, keepdims=True))
               + lam_init)
        o_t = o0 - lam * o1
        var = jnp.mean(o_t * o_t, axis=0, keepdims=True)
        y_t = o_t * lax.rsqrt(var + EPS) * g_ref[...] * (1.0 - lam_init)
        o_ref[...] = (y_t.T * sz).astype(BF16)


def _attention(q, k, vt, sz, *, fox, f=None, lam_vecs=None, subln_g=None, lam_init=0.0, sz_offset=0):
    B, S, _ = q.shape
    T = ATTN_BLOCK
    nk = S // T
    nq = S // Q_BLOCK
    qspec = pl.BlockSpec((None, Q_BLOCK, LANES), lambda b, u, i: (b, i, u))
    kspec = pl.BlockSpec((None, S, LANES), lambda b, u, i: (b, 0, u))
    vspec = pl.BlockSpec((None, None, nk, vt.shape[3], T), lambda b, u, i: (b, u, 0, 0, 0))
    szspec = pl.BlockSpec((None, Q_BLOCK, LANES), lambda b, u, i: (b, i, u + sz_offset))
    if fox:
        extra = [f]
        extra_specs = [pl.BlockSpec((None, None, S, 2), lambda b, u, i: (b, u, 0, 0),
                                    pipeline_mode=pl.Buffered(1))]
        scratch = [pltpu.VMEM((2, FOX_MAP_ROWS, Q_BLOCK), F32)]
    else:
        extra = [lam_vecs, subln_g]
        extra_specs = [pl.BlockSpec((4, HEAD_DIM), lambda b, u, i: (0, 0)),
                       pl.BlockSpec((LANES, 1), lambda b, u, i: (0, 0))]
        scratch = [pltpu.VMEM((2, DIFF_MAP_ROWS, Q_BLOCK), F32)]
    return pl.pallas_call(
        functools.partial(_attention_kernel, fox=fox, lam_init=lam_init),
        grid=(B, N_UNITS, nq),
        in_specs=[qspec, kspec, vspec] + extra_specs + [szspec],
        out_specs=pl.BlockSpec((None, Q_BLOCK, LANES), lambda b, u, i: (b, i, u)),
        out_shape=jax.ShapeDtypeStruct((B, S, N_UNITS * LANES), BF16),
        scratch_shapes=scratch + [pltpu.VMEM((2, T, Q_BLOCK), F32)] * 2,
        compiler_params=pltpu.CompilerParams(
            dimension_semantics=("arbitrary", "arbitrary", "arbitrary"),
            vmem_limit_bytes=VMEM_LIMIT),
        name="fox_attention" if fox else "diff_attention",
    )(q, k, vt, *extra, sz)


def _mid_proj_kernel(x_ref, yf_ref, yd_ref, wo_ref, g_ref, w_ref, wg_ref, bg_ref,
                     x1_ref, q_ref, k_ref, vt_ref, gates_ref, gate_ref):
    x1 = (x_ref[...]
          + jnp.dot(yf_ref[...], wo_ref[0:FOX_WIDTH, :], preferred_element_type=F32)
          + jnp.dot(yd_ref[...], wo_ref[FOX_WIDTH:ATTN_WIDTH, :], preferred_element_type=F32))
    x1_ref[...] = x1
    hb = _rms(x1, g_ref[...]).astype(BF16)

    def proj(lo, hi):
        return jnp.dot(hb, w_ref[:, lo:hi], preferred_element_type=F32)

    q_ref[...] = proj(0, 512).astype(BF16)
    k_ref[...] = proj(512, 1024).astype(BF16)
    v = proj(1024, 2048)
    ones = jnp.ones((ONES_ROWS, ROW_BLOCK), BF16)
    for h in range(ML_HEADS):
        base = h * ML_VT_ROWS
        vt_ref[base:base + ML_V_DIM, :] = v[:, h * ML_V_DIM:(h + 1) * ML_V_DIM].T.astype(BF16)
        vt_ref[base + ML_V_DIM:base + ML_VT_ROWS, :] = ones
    og = proj(2048, 3072)
    z = proj(3072, 4096)
    gate_ref[...] = (_sigmoid(og) * (z * _sigmoid(z))).astype(BF16)

    gt = jnp.dot(hb, wg_ref[...], preferred_element_type=F32) + bg_ref[...]
    lane = lax.broadcasted_iota(jnp.int32, (1, LANES), 1)
    gates_ref[...] = jnp.where(lane < ML_HEADS, gt, _log_sigmoid(gt))


def _mid_proj(x, yf, yd, wo, g, w, wg, bg):
    B, S, _ = x.shape
    T = ROW_BLOCK
    row = lambda b, i: (b, i, 0)
    const2 = lambda b, i: (0, 0)
    return pl.pallas_call(
        _mid_proj_kernel,
        grid=(B, S // T),
        in_specs=[
            pl.BlockSpec((None, T, D_MODEL), row),
            pl.BlockSpec((None, T, FOX_WIDTH), row),
            pl.BlockSpec((None, T, DIFF_WIDTH), row),
            pl.BlockSpec((ATTN_WIDTH, D_MODEL), const2),
            pl.BlockSpec((1, D_MODEL), const2),
            pl.BlockSpec((D_MODEL, 4096), const2),
            pl.BlockSpec((D_MODEL, LANES), const2),
            pl.BlockSpec((1, LANES), const2),
        ],
        out_specs=(
            pl.BlockSpec((None, T, D_MODEL), row),
            pl.BlockSpec((None, T, ML_QK_WIDTH), row),
            pl.BlockSpec((None, T, ML_QK_WIDTH), row),
            pl.BlockSpec((None, ML_HEADS * ML_VT_ROWS, T), lambda b, i: (b, 0, i)),
            pl.BlockSpec((None, T, LANES), row),
            pl.BlockSpec((None, T, ML_WIDTH), row),
        ),
        out_shape=(
            jax.ShapeDtypeStruct((B, S, D_MODEL), F32),
            jax.ShapeDtypeStruct((B, S, ML_QK_WIDTH), BF16),
            jax.ShapeDtypeStruct((B, S, ML_QK_WIDTH), BF16),
            jax.ShapeDtypeStruct((B, ML_HEADS * ML_VT_ROWS, S), BF16),
            jax.ShapeDtypeStruct((B, S, LANES), F32),
            jax.ShapeDtypeStruct((B, S, ML_WIDTH), BF16),
        ),
        compiler_params=pltpu.CompilerParams(
            dimension_semantics=("arbitrary", "arbitrary"), vmem_limit_bytes=VMEM_LIMIT),
        name="mid_proj",
    )(x, yf, yd, wo, g, w, wg, bg)


def _mlstm_kernel(q_ref, k_ref, vt_ref, gates_ref, gate_ref, o_ref, ct_ref, m_ref):
    L = ML_CHUNK
    NT = (((1,), (1,)), ((), ()))

    @pl.when(pl.program_id(1) == 0)
    def _():
        ct_ref[...] = jnp.zeros_like(ct_ref)
        m_ref[...] = jnp.zeros_like(m_ref)

    gts = gates_ref[...]
    lane = lax.broadcasted_iota(jnp.int32, (1, LANES), 1)
    cols = jnp.where(lane < ML_HEADS, gts, _tri_cumsum(gts))
    rows = cols.T
    allowed = (lax.broadcasted_iota(jnp.int32, (L, L), 0)
               <= lax.broadcasted_iota(jnp.int32, (L, L), 1))

    def decay_and_scores(h):
        li_row = rows[h:h + 1, :]
        b_row = rows[ML_HEADS + h:ML_HEADS + h + 1, :]
        u_col = cols[:, h:h + 1] - cols[:, ML_HEADS + h:ML_HEADS + h + 1]
        a = b_row[:, L - 1:L]
        m_prev = m_ref[:, h:h + 1]
        log_d = jnp.where(allowed, u_col + b_row, NEG_INF)
        log_inter = b_row + m_prev
        m_t = jnp.maximum(log_inter, jnp.max(log_d, axis=0, keepdims=True))
        d = jnp.exp(log_d - m_t)
        w_inter = jnp.exp(log_inter - m_t)
        qh = q_ref[:, h * ML_QK_DIM:(h + 1) * ML_QK_DIM]
        kh = k_ref[:, h * ML_QK_DIM:(h + 1) * ML_QK_DIM]
        ct = ct_ref[h]
        kq = lax.dot_general(kh, qh, NT, preferred_element_type=F32)
        ctq = lax.dot_general(ct.astype(BF16), qh, NT, preferred_element_type=F32)
        return dict(li_row=li_row, b_row=b_row, a=a, m_prev=m_prev, m_t=m_t, d=d,
                    w_inter=w_inter, kh=kh, ct=ct, kq=kq, ctq=ctq)

    def outputs_and_state(h, st):
        vt = vt_ref[h * ML_VT_ROWS:(h + 1) * ML_VT_ROWS, :]
        s = (st["kq"] * st["d"]).astype(BF16)
        tot = st["w_inter"] * st["ctq"] + jnp.dot(vt, s, preferred_element_type=F32)
        den = tot[ML_V_DIM:ML_V_DIM + 1, :]
        h_t = tot[0:ML_V_DIM, :] / jnp.maximum(jnp.abs(den), jnp.exp(-st["m_t"]))
        sl = slice(h * ML_V_DIM, (h + 1) * ML_V_DIM)
        o_ref[:, sl] = (h_t.T * gate_ref[:, sl].astype(F32)).astype(BF16)

        g_row = st["a"] - st["b_row"] + st["li_row"]
        m_new = jnp.maximum(st["a"] + st["m_prev"], jnp.max(g_row, axis=1, keepdims=True))
        w_old = jnp.exp(st["a"] + st["m_prev"] - m_new)
        vw = (vt.astype(F32) * jnp.exp(g_row - m_new)).astype(BF16)
        ct_ref[h] = w_old * st["ct"] + jnp.dot(vw, st["kh"], preferred_element_type=F32)
        m_ref[:, h:h + 1] = m_new

    ahead = decay_and_scores(0)
    for h in range(ML_HEADS):
        cur = ahead
        if h + 1 < ML_HEADS:
            ahead = decay_and_scores(h + 1)
        outputs_and_state(h, cur)


def _mlstm(q, k, vt, gates, gate):
    B, S, _ = q.shape
    L = ML_CHUNK
    row = lambda b, c: (b, c, 0)
    return pl.pallas_call(
        _mlstm_kernel,
        grid=(B, S // L),
        in_specs=[
            pl.BlockSpec((None, L, ML_QK_WIDTH), row),
            pl.BlockSpec((None, L, ML_QK_WIDTH), row),
            pl.BlockSpec((None, ML_HEADS * ML_VT_ROWS, L), lambda b, c: (b, 0, c)),
            pl.BlockSpec((None, L, LANES), row),
            pl.BlockSpec((None, L, ML_WIDTH), row),
        ],
        out_specs=pl.BlockSpec((None, L, ML_WIDTH), row),
        out_shape=jax.ShapeDtypeStruct((B, S, ML_WIDTH), BF16),
        scratch_shapes=[pltpu.VMEM((ML_HEADS, ML_VT_ROWS, ML_QK_DIM), F32),
                        pltpu.VMEM((1, LANES), F32)],
        compiler_params=pltpu.CompilerParams(
            dimension_semantics=("arbitrary", "arbitrary"), vmem_limit_bytes=VMEM_LIMIT),
        name="mlstm",
    )(q, k, vt, gates, gate)


def _final_proj_kernel(x_ref, y_ref, wo_ref, g_ref, o_ref):
    x2 = x_ref[...] + jnp.dot(y_ref[...], wo_ref[...], preferred_element_type=F32)
    o_ref[...] = _rms(x2, g_ref[...])


def _final_proj(x1, y, wo, g):
    B, S, _ = x1.shape
    T = ROW_BLOCK
    row = lambda b, i: (b, i, 0)
    const2 = lambda b, i: (0, 0)
    return pl.pallas_call(
        _final_proj_kernel,
        grid=(B, S // T),
        in_specs=[
            pl.BlockSpec((None, T, D_MODEL), row),
            pl.BlockSpec((None, T, ML_WIDTH), row),
            pl.BlockSpec((ML_WIDTH, D_MODEL), const2),
            pl.BlockSpec((1, D_MODEL), const2),
        ],
        out_specs=pl.BlockSpec((None, T, D_MODEL), row),
        out_shape=jax.ShapeDtypeStruct((B, S, D_MODEL), F32),
        compiler_params=pltpu.CompilerParams(
            dimension_semantics=("arbitrary", "arbitrary"), vmem_limit_bytes=VMEM_LIMIT),
        name="final_proj",
    )(x1, y, wo, g)


def _rope_lane_perm():
    n = np.arange(LANES)
    half, c, r = n // 64, (n % 64) // 32, n % 32
    per_head = c * HEAD_DIM + half * 32 + r
    return np.concatenate([h * LANES + per_head for h in range(DIFF_HEADS)])


def _pad_lanes(t):
    return jnp.pad(t, ((0, 0), (0, LANES - t.shape[1])))


def kernel(x, positions, attn_norm_g, attn_w_in, fox_b_f, diff_lam_q1, diff_lam_k1, diff_lam_q2,
           diff_lam_k2, diff_subln_g, attn_w_out, ml_norm_g, ml_w_in, ml_b_i, ml_b_f, ml_w_out,
           final_norm_g):
    B, S, _ = x.shape
    scale = HEAD_DIM ** -0.5 * LOG2E

    w = attn_w_in[0]
    o = np.cumsum((0, FOX_WIDTH, FOX_WIDTH, FOX_WIDTH, FOX_HEADS, DIFF_WIDTH, DIFF_WIDTH, DIFF_WIDTH,
                   ATTN_WIDTH))
    perm = _rope_lane_perm()
    w_main = jnp.concatenate([
        w[:, o[0]:o[1]] * scale, w[:, o[1]:o[2]], w[:, o[2]:o[3]],
        w[:, o[4]:o[5]][:, perm] * scale, w[:, o[5]:o[6]][:, perm], w[:, o[6]:o[7]],
        w[:, o[7]:o[8]]], axis=1).astype(BF16)
    w_f = _pad_lanes(w[:, o[3]:o[4]]).astype(BF16)
    b_f = _pad_lanes(fox_b_f[0][None, :])
    half = HEAD_DIM // 2
    inv = ROPE_THETA ** (-jnp.arange(half, dtype=F32) / half)
    inv = jnp.tile(inv, LANES // half)[None, :]
    posf = positions.astype(F32)[..., None]

    qf, kf, vft, fcum, qd, kd, vdt, sz = _attn_proj(
        x, posf, attn_norm_g[0][None, :], w_main, w_f, b_f, inv)

    y_fox = _attention(qf, kf, vft, sz, fox=True, f=fcum, sz_offset=0)
    lam_vecs = jnp.stack([diff_lam_q1[0], diff_lam_k1[0], diff_lam_q2[0], diff_lam_k2[0]])
    lam_init = 0.8 - 0.6 * math.exp(-0.3 * 0)
    y_diff = _attention(qd, kd, vdt, sz, fox=False, lam_vecs=lam_vecs,
                        subln_g=diff_subln_g[0][:, None], lam_init=lam_init, sz_offset=N_UNITS)

    w2 = ml_w_in[0]
    o2 = np.cumsum((0, ML_QK_WIDTH, ML_QK_WIDTH, ML_WIDTH, ML_HEADS, ML_HEADS, ML_WIDTH, ML_WIDTH))
    w2_main = jnp.concatenate([
        w2[:, o2[0]:o2[1]], w2[:, o2[1]:o2[2]] * (ML_QK_DIM ** -0.5), w2[:, o2[2]:o2[3]],
        w2[:, o2[5]:o2[6]], w2[:, o2[6]:o2[7]]], axis=1).astype(BF16)
    w2_g = _pad_lanes(w2[:, o2[3]:o2[5]]).astype(BF16)
    b_g = _pad_lanes(jnp.concatenate([ml_b_i[0], ml_b_f[0]])[None, :])

    x1, q2, k2, v2t, gates, gate = _mid_proj(
        x, y_fox, y_diff, attn_w_out[0].astype(BF16), ml_norm_g[0][None, :], w2_main, w2_g, b_g)
    y2 = _mlstm(q2, k2, v2t, gates, gate)
    return _final_proj(x1, y2, ml_w_out[0].astype(BF16), final_norm_g[None, :])
```

```python
import functools
import math

import numpy as np
import jax
import jax.numpy as jnp
from jax import lax
from jax.experimental import pallas as pl
from jax.experimental.pallas import tpu as pltpu

F32 = jnp.float32
BF16 = jnp.bfloat16

D_MODEL = 1024
HEAD_DIM = 64
FOX_HEADS = 8
DIFF_HEADS = 4
FOX_WIDTH = FOX_HEADS * HEAD_DIM
DIFF_WIDTH = DIFF_HEADS * 2 * HEAD_DIM
ATTN_WIDTH = FOX_WIDTH + DIFF_WIDTH
ML_HEADS = 8
ML_QK_DIM = 64
ML_V_DIM = 128
ML_QK_WIDTH = ML_HEADS * ML_QK_DIM
ML_WIDTH = ML_HEADS * ML_V_DIM
ROPE_THETA = 10000.0
EPS = 1e-6

LANES = 128
N_UNITS = 4
ATTN_BLOCK = 512
Q_BLOCK = 2048
Q_SUB = 256
ONES_ROWS = 16
FOX_MAP_ROWS = HEAD_DIM + ONES_ROWS
DIFF_MAP_ROWS = 2 * HEAD_DIM + ONES_ROWS
LOG2E = math.log2(math.e)
ML_CHUNK = 256
ML_VT_ROWS = ML_V_DIM + ONES_ROWS
ROW_BLOCK = 512
FINAL_BLOCK = 1024
VMEM_LIMIT = 56 * 1024 * 1024

NEG_INF = float("-inf")


def _split3(x):
    hi = x.astype(BF16)
    r1 = x - hi.astype(F32)
    mid = r1.astype(BF16)
    lo = (r1 - mid.astype(F32)).astype(BF16)
    return hi, mid, lo


def _tri_cumsum(x):
    n = x.shape[0]
    row = lax.broadcasted_iota(jnp.int32, (n, n), 0)
    col = lax.broadcasted_iota(jnp.int32, (n, n), 1)
    tri = jnp.where(row >= col, 1.0, 0.0).astype(BF16)
    hi, mid, lo = _split3(x)
    return (jnp.dot(tri, hi, preferred_element_type=F32)
            + jnp.dot(tri, mid, preferred_element_type=F32)
            + jnp.dot(tri, lo, preferred_element_type=F32))


def _log_sigmoid(x):
    return jnp.minimum(x, 0.0) - jnp.log(1.0 + jnp.exp(-jnp.abs(x)))


def _sigmoid(x):
    return 1.0 / (1.0 + jnp.exp(-x))


def _rms(x, g):
    return x * lax.rsqrt(jnp.mean(x * x, axis=-1, keepdims=True) + EPS) * g


def _attn_proj_kernel(x_ref, pos_ref, g_ref, w_ref, wf_ref, bf_ref, inv_ref,
                      qf_ref, kf_ref, vft_ref, f_ref, qd_ref, kd_ref, vdt_ref, sz_ref,
                      carry_ref):
    @pl.when(pl.program_id(1) == 0)
    def _():
        carry_ref[...] = jnp.zeros_like(carry_ref)

    H = ROW_BLOCK // 2
    halves = [slice(i * H, (i + 1) * H) for i in range(2)]
    hbs = [_rms(x_ref[r, :], g_ref[...]).astype(BF16) for r in halves]
    lane = lax.broadcasted_iota(jnp.int32, (1, LANES), 1)
    ones = jnp.ones((ONES_ROWS, H), BF16)

    for r, hb in zip(halves, hbs):
        def proj(lo, hi):
            return jnp.dot(hb, w_ref[:, lo:hi], preferred_element_type=F32)

        qf_ref[r, :] = proj(0, 512).astype(BF16)
        kf_ref[r, :] = proj(512, 1024).astype(BF16)
        vf = proj(1024, 1536)
        for u in range(N_UNITS):
            t = vf[:, u * LANES:(u + 1) * LANES].T.astype(BF16)
            for c in range(2):
                base = c * FOX_MAP_ROWS
                vft_ref[u, base:base + HEAD_DIM, r] = t[c * HEAD_DIM:(c + 1) * HEAD_DIM, :]
                vft_ref[u, base + HEAD_DIM:base + FOX_MAP_ROWS, r] = ones

        ang = pos_ref[r, :] * inv_ref[...]
        cos = jnp.cos(ang)
        sin = jnp.sin(ang)
        sin_signed = jnp.where(lane < 64, -sin, sin)

        def rope_store(lo, out_ref):
            t = proj(lo, lo + 512)
            for h in range(DIFF_HEADS):
                xh = t[:, h * LANES:(h + 1) * LANES]
                out_ref[r, h * LANES:(h + 1) * LANES] = (
                    xh * cos + pltpu.roll(xh, 64, 1) * sin_signed).astype(BF16)

        rope_store(1536, qd_ref)
        rope_store(2048, kd_ref)
        vd = proj(2560, 3072)
        for u in range(N_UNITS):
            vdt_ref[u, 0:LANES, r] = vd[:, u * LANES:(u + 1) * LANES].T.astype(BF16)
            vdt_ref[u, LANES:DIFF_MAP_ROWS, r] = ones

        z = proj(3072, 4096)
        sz_ref[r, :] = (z * _sigmoid(z)).astype(BF16)

        lf = _log_sigmoid(jnp.dot(hb, wf_ref[...], preferred_element_type=F32) + bf_ref[...])
        cs = _tri_cumsum(lf) + carry_ref[...]
        carry_ref[...] = cs[H - 1:H, :]
        for u in range(N_UNITS):
            f_ref[u, r, :] = cs[:, 2 * u:2 * u + 2] * LOG2E


def _attn_proj(x, posf, g, w, wf, bf, inv):
    B, S, _ = x.shape
    T = ROW_BLOCK
    nb = S // T
    row = lambda b, i: (b, i, 0)
    const2 = lambda b, i: (0, 0)
    vt_spec = lambda rows: pl.BlockSpec((None, N_UNITS, None, rows, T), lambda b, i: (b, 0, i, 0, 0))
    out_shape = (
        jax.ShapeDtypeStruct((B, S, 512), BF16),
        jax.ShapeDtypeStruct((B, S, 512), BF16),
        jax.ShapeDtypeStruct((B, N_UNITS, nb, 2 * FOX_MAP_ROWS, T), BF16),
        jax.ShapeDtypeStruct((B, N_UNITS, S, 2), F32),
        jax.ShapeDtypeStruct((B, S, 512), BF16),
        jax.ShapeDtypeStruct((B, S, 512), BF16),
        jax.ShapeDtypeStruct((B, N_UNITS, nb, DIFF_MAP_ROWS, T), BF16),
        jax.ShapeDtypeStruct((B, S, ATTN_WIDTH), BF16),
    )
    return pl.pallas_call(
        _attn_proj_kernel,
        grid=(B, nb),
        in_specs=[
            pl.BlockSpec((None, T, D_MODEL), row),
            pl.BlockSpec((None, T, 1), row),
            pl.BlockSpec((1, D_MODEL), const2),
            pl.BlockSpec((D_MODEL, 4096), const2),
            pl.BlockSpec((D_MODEL, LANES), const2),
            pl.BlockSpec((1, LANES), const2),
            pl.BlockSpec((1, LANES), const2),
        ],
        out_specs=(
            pl.BlockSpec((None, T, 512), row),
            pl.BlockSpec((None, T, 512), row),
            vt_spec(2 * FOX_MAP_ROWS),
            pl.BlockSpec((None, N_UNITS, T, 2), lambda b, i: (b, 0, i, 0)),
            pl.BlockSpec((None, T, 512), row),
            pl.BlockSpec((None, T, 512), row),
            vt_spec(DIFF_MAP_ROWS),
            pl.BlockSpec((None, T, ATTN_WIDTH), row),
        ),
        out_shape=out_shape,
        scratch_shapes=[pltpu.VMEM((1, LANES), F32)],
        compiler_params=pltpu.CompilerParams(
            dimension_semantics=("arbitrary", "arbitrary"), vmem_limit_bytes=VMEM_LIMIT),
        name="attn_proj",
    )(x, posf, g, w, wf, bf, inv)


def _attention_kernel(*refs, fox, lam_init):
    if fox:
        q_ref, k_ref, vt_ref, f_ref, sz_ref, o_ref, acc_ref, sa_ref, sb_ref = refs
    else:
        q_ref, k_ref, vt_ref, lam_ref, g_ref, sz_ref, o_ref, acc_ref, sa_ref, sb_ref = refs
    T = ATTN_BLOCK
    n_sub = Q_BLOCK // Q_SUB
    vrows = FOX_MAP_ROWS if fox else DIFF_MAP_ROWS
    vdim = vrows - ONES_ROWS
    qi = pl.program_id(2)
    kb_diag = (Q_BLOCK // T) * qi

    q = q_ref[...]
    lane = lax.broadcasted_iota(jnp.int32, (1, LANES), 1)
    first = (lane < 64) if fox else ((lane % 64) < 32)
    zero = jnp.zeros_like(q)
    qm = (jnp.where(first, q, zero), jnp.where(first, zero, q))
    qm = [[qc[j * Q_SUB:(j + 1) * Q_SUB, :] for j in range(n_sub)] for qc in qm]

    acc_ref[...] = jnp.zeros_like(acc_ref)
    if fox:
        q0 = pl.multiple_of(qi * Q_BLOCK, Q_BLOCK)
        f0 = [f_ref[pl.ds(q0, 1), c:c + 1] for c in range(2)]

    chains = [(c, j) for c in range(2) for j in range(n_sub)]
    n_diag = Q_BLOCK // T
    per_blk = n_sub // n_diag

    def scores(kb, s_ref, diag=None):
        start = pl.multiple_of(kb * T, T)
        k = k_ref[pl.ds(start, T), :]
        if fox:
            bias = [f_ref[pl.ds(start, T), c:c + 1] - f0[c] for c in range(2)]
        if diag is not None:
            kpos = lax.broadcasted_iota(jnp.int32, (T, Q_SUB), 0)
            qpos = lax.broadcasted_iota(jnp.int32, (T, Q_SUB), 1)
        for c, j in chains:
            if diag is not None and j // per_blk < diag:
                yield None
                continue
            sc = lax.dot_general(k, qm[c][j], (((1,), (1,)), ((), ())),
                                 preferred_element_type=F32)
            if fox:
                sc = sc - bias[c]
            if diag is not None and j // per_blk == diag:
                sc = jnp.where(kpos <= qpos + (j % per_blk) * Q_SUB, sc, NEG_INF)
            s_ref[c, :, j * Q_SUB:(j + 1) * Q_SUB] = sc
            yield jnp.max(sc, axis=0, keepdims=True)

    def accumulate(kb, s_ref, mb, m):
        for i, (c, j) in enumerate(chains):
            if mb[i] is None:
                yield m[i]
                continue
            lanes = slice(j * Q_SUB, (j + 1) * Q_SUB)
            m_new = jnp.maximum(m[i], mb[i])
            alpha = jnp.exp2(m[i] - m_new)
            pb = jnp.exp2(s_ref[c, :, lanes] - m_new).astype(BF16)
            vt = vt_ref[kb, 0:vrows, :] if not fox else vt_ref[kb, c * vrows:(c + 1) * vrows, :]
            acc_ref[c, :, lanes] = alpha * acc_ref[c, :, lanes] + jnp.dot(
                vt, pb, preferred_element_type=F32)
            yield m_new

    def overlapped(next_scores, kb_cur, s_cur, mb_cur, m):
        mb_next, m_new = [], []
        for a, b in zip(next_scores, accumulate(kb_cur, s_cur, mb_cur, m)):
            mb_next.append(a)
            m_new.append(b)
        return tuple(mb_next), tuple(m_new)

    bufs = (sa_ref, sb_ref)
    m = (jnp.full((1, Q_SUB), NEG_INF, F32),) * len(chains)
    mb = tuple(scores(kb_diag + n_diag - 1, bufs[0], diag=n_diag - 1))
    for i in range(1, n_diag):
        d = n_diag - 1 - i
        mb, m = overlapped(scores(kb_diag + d, bufs[i % 2], diag=d),
                           kb_diag + d + 1, bufs[(i - 1) % 2], mb, m)

    def pair(jj, carry):
        mb_b, m = carry
        kb_b = jnp.where(jj == 0, kb_diag, 2 * jj - 1)
        mb_a, m = overlapped(scores(2 * jj, sa_ref), kb_b, sb_ref, mb_b, m)
        mb_b, m = overlapped(scores(2 * jj + 1, sb_ref), 2 * jj, sa_ref, mb_a, m)
        return mb_b, m

    mb, m = lax.fori_loop(0, kb_diag // 2, pair, (mb, m))
    tuple(accumulate(jnp.where(qi == 0, kb_diag, kb_diag - 1), sb_ref, mb, m))

    sz = sz_ref[...].astype(F32)
    o0 = acc_ref[0, 0:vdim, :] / acc_ref[0, vdim:vdim + 1, :]
    o1 = acc_ref[1, 0:vdim, :] / acc_ref[1, vdim:vdim + 1, :]
    if fox:
        o_t = jnp.concatenate([o0, o1], axis=0)
        o_ref[...] = (o_t.T * sz).astype(BF16)
    else:
        lam = (jnp.exp(jnp.sum(lam_ref[0:1, :] * lam_ref[1:2, :], axis=1, keepdims=True))
               - jnp.exp(jnp.sum(lam_ref[2:3, :] * lam_ref[3:4, :], axis=1, keepdims=True))
               + lam_init)
        o_t = o0 - lam * o1
        var = jnp.mean(o_t * o_t, axis=0, keepdims=True)
        y_t = o_t * lax.rsqrt(var + EPS) * g_ref[...] * (1.0 - lam_init)
        o_ref[...] = (y_t.T * sz).astype(BF16)


def _attention(q, k, vt, sz, *, fox, f=None, lam_vecs=None, subln_g=None, lam_init=0.0, sz_offset=0):
    B, S, _ = q.shape
    T = ATTN_BLOCK
    nk = S // T
    nq = S // Q_BLOCK
    qspec = pl.BlockSpec((None, Q_BLOCK, LANES), lambda b, u, i: (b, i, u))
    kspec = pl.BlockSpec((None, S, LANES), lambda b, u, i: (b, 0, u))
    vspec = pl.BlockSpec((None, None, nk, vt.shape[3], T), lambda b, u, i: (b, u, 0, 0, 0))
    szspec = pl.BlockSpec((None, Q_BLOCK, LANES), lambda b, u, i: (b, i, u + sz_offset))
    if fox:
        extra = [f]
        extra_specs = [pl.BlockSpec((None, None, S, 2), lambda b, u, i: (b, u, 0, 0),
                                    pipeline_mode=pl.Buffered(1))]
        scratch = [pltpu.VMEM((2, FOX_MAP_ROWS, Q_BLOCK), F32)]
    else:
        extra = [lam_vecs, subln_g]
        extra_specs = [pl.BlockSpec((4, HEAD_DIM), lambda b, u, i: (0, 0)),
                       pl.BlockSpec((LANES, 1), lambda b, u, i: (0, 0))]
        scratch = [pltpu.VMEM((2, DIFF_MAP_ROWS, Q_BLOCK), F32)]
    return pl.pallas_call(
        functools.partial(_attention_kernel, fox=fox, lam_init=lam_init),
        grid=(B, N_UNITS, nq),
        in_specs=[qspec, kspec, vspec] + extra_specs + [szspec],
        out_specs=pl.BlockSpec((None, Q_BLOCK, LANES), lambda b, u, i: (b, i, u)),
        out_shape=jax.ShapeDtypeStruct((B, S, N_UNITS * LANES), BF16),
        scratch_shapes=scratch + [pltpu.VMEM((2, T, Q_BLOCK), F32)] * 2,
        compiler_params=pltpu.CompilerParams(
            dimension_semantics=("arbitrary", "arbitrary", "arbitrary"),
            vmem_limit_bytes=VMEM_LIMIT),
        name="fox_attention" if fox else "diff_attention",
    )(q, k, vt, *extra, sz)


def _mid_proj_kernel(x_ref, yf_ref, yd_ref, wo_ref, g_ref, w_ref, wg_ref, bg_ref,
                     x1_ref, q_ref, k_ref, vt_ref, gates_ref, gate_ref):
    H = ROW_BLOCK // 2
    halves = [slice(i * H, (i + 1) * H) for i in range(2)]
    x1 = []
    for r in halves:
        x1.append(x_ref[r, :]
                  + jnp.dot(yf_ref[r, :], wo_ref[0:FOX_WIDTH, :], preferred_element_type=F32)
                  + jnp.dot(yd_ref[r, :], wo_ref[FOX_WIDTH:ATTN_WIDTH, :], preferred_element_type=F32))
    lane = lax.broadcasted_iota(jnp.int32, (1, LANES), 1)
    ones = jnp.ones((ONES_ROWS, H), BF16)
    for r, xh in zip(halves, x1):
        x1_ref[r, :] = xh
        hb = _rms(xh, g_ref[...]).astype(BF16)

        def proj(lo, hi):
            return jnp.dot(hb, w_ref[:, lo:hi], preferred_element_type=F32)

        q_ref[r, :] = proj(0, 512).astype(BF16)
        k_ref[r, :] = proj(512, 1024).astype(BF16)
        v = proj(1024, 2048)
        for h in range(ML_HEADS):
            base = h * ML_VT_ROWS
            vt_ref[base:base + ML_V_DIM, r] = v[:, h * ML_V_DIM:(h + 1) * ML_V_DIM].T.astype(BF16)
            vt_ref[base + ML_V_DIM:base + ML_VT_ROWS, r] = ones
        og = proj(2048, 3072)
        z = proj(3072, 4096)
        gate_ref[r, :] = (_sigmoid(og) * (z * _sigmoid(z))).astype(BF16)

        gt = jnp.dot(hb, wg_ref[...], preferred_element_type=F32) + bg_ref[...]
        gates_ref[r, :] = jnp.where(lane < ML_HEADS, gt, _log_sigmoid(gt))


def _mid_proj(x, yf, yd, wo, g, w, wg, bg):
    B, S, _ = x.shape
    T = ROW_BLOCK
    row = lambda b, i: (b, i, 0)
    const2 = lambda b, i: (0, 0)
    return pl.pallas_call(
        _mid_proj_kernel,
        grid=(B, S // T),
        in_specs=[
            pl.BlockSpec((None, T, D_MODEL), row),
            pl.BlockSpec((None, T, FOX_WIDTH), row),
            pl.BlockSpec((None, T, DIFF_WIDTH), row),
            pl.BlockSpec((ATTN_WIDTH, D_MODEL), const2),
            pl.BlockSpec((1, D_MODEL), const2),
            pl.BlockSpec((D_MODEL, 4096), const2),
            pl.BlockSpec((D_MODEL, LANES), const2),
            pl.BlockSpec((1, LANES), const2),
        ],
        out_specs=(
            pl.BlockSpec((None, T, D_MODEL), row),
            pl.BlockSpec((None, T, ML_QK_WIDTH), row),
            pl.BlockSpec((None, T, ML_QK_WIDTH), row),
            pl.BlockSpec((None, ML_HEADS * ML_VT_ROWS, T), lambda b, i: (b, 0, i)),
            pl.BlockSpec((None, T, LANES), row),
            pl.BlockSpec((None, T, ML_WIDTH), row),
        ),
        out_shape=(
            jax.ShapeDtypeStruct((B, S, D_MODEL), F32),
            jax.ShapeDtypeStruct((B, S, ML_QK_WIDTH), BF16),
            jax.ShapeDtypeStruct((B, S, ML_QK_WIDTH), BF16),
            jax.ShapeDtypeStruct((B, ML_HEADS * ML_VT_ROWS, S), BF16),
            jax.ShapeDtypeStruct((B, S, LANES), F32),
            jax.ShapeDtypeStruct((B, S, ML_WIDTH), BF16),
        ),
        compiler_params=pltpu.CompilerParams(
            dimension_semantics=("arbitrary", "arbitrary"), vmem_limit_bytes=VMEM_LIMIT),
        name="mid_proj",
    )(x, yf, yd, wo, g, w, wg, bg)


def _mlstm_kernel(q_ref, k_ref, vt_ref, gates_ref, gate_ref, o_ref, ct_ref, m_ref):
    L = ML_CHUNK
    NT = (((1,), (1,)), ((), ()))

    @pl.when(pl.program_id(1) == 0)
    def _():
        ct_ref[...] = jnp.zeros_like(ct_ref)
        m_ref[...] = jnp.zeros_like(m_ref)

    gts = gates_ref[...]
    lane = lax.broadcasted_iota(jnp.int32, (1, LANES), 1)
    cols = jnp.where(lane < ML_HEADS, gts, _tri_cumsum(gts))
    rows = cols.T
    allowed = (lax.broadcasted_iota(jnp.int32, (L, L), 0)
               <= lax.broadcasted_iota(jnp.int32, (L, L), 1))

    def decay_and_scores(h):
        li_row = rows[h:h + 1, :]
        b_row = rows[ML_HEADS + h:ML_HEADS + h + 1, :]
        u_col = cols[:, h:h + 1] - cols[:, ML_HEADS + h:ML_HEADS + h + 1]
        a = b_row[:, L - 1:L]
        m_prev = m_ref[:, h:h + 1]
        log_d = jnp.where(allowed, u_col + b_row, NEG_INF)
        log_inter = b_row + m_prev
        m_t = jnp.maximum(log_inter, jnp.max(log_d, axis=0, keepdims=True))
        d = jnp.exp(log_d - m_t)
        w_inter = jnp.exp(log_inter - m_t)
        qh = q_ref[:, h * ML_QK_DIM:(h + 1) * ML_QK_DIM]
        kh = k_ref[:, h * ML_QK_DIM:(h + 1) * ML_QK_DIM]
        ct = ct_ref[h]
        kq = lax.dot_general(kh, qh, NT, preferred_element_type=F32)
        ctq = lax.dot_general(ct.astype(BF16), qh, NT, preferred_element_type=F32)
        return dict(li_row=li_row, b_row=b_row, a=a, m_prev=m_prev, m_t=m_t, d=d,
                    w_inter=w_inter, kh=kh, ct=ct, kq=kq, ctq=ctq)

    def outputs_and_state(h, st):
        vt = vt_ref[h * ML_VT_ROWS:(h + 1) * ML_VT_ROWS, :]
        s = (st["kq"] * st["d"]).astype(BF16)
        tot = st["w_inter"] * st["ctq"] + jnp.dot(vt, s, preferred_element_type=F32)
        den = tot[ML_V_DIM:ML_V_DIM + 1, :]
        h_t = tot[0:ML_V_DIM, :] / jnp.maximum(jnp.abs(den), jnp.exp(-st["m_t"]))
        sl = slice(h * ML_V_DIM, (h + 1) * ML_V_DIM)
        o_ref[:, sl] = (h_t.T * gate_ref[:, sl].astype(F32)).astype(BF16)

        g_row = st["a"] - st["b_row"] + st["li_row"]
        m_new = jnp.maximum(st["a"] + st["m_prev"], jnp.max(g_row, axis=1, keepdims=True))
        w_old = jnp.exp(st["a"] + st["m_prev"] - m_new)
        vw = (vt.astype(F32) * jnp.exp(g_row - m_new)).astype(BF16)
        ct_ref[h] = w_old * st["ct"] + jnp.dot(vw, st["kh"], preferred_element_type=F32)
        m_ref[:, h:h + 1] = m_new

    ahead = decay_and_scores(0)
    for h in range(ML_HEADS):
        cur = ahead
        if h + 1 < ML_HEADS:
            ahead = decay_and_scores(h + 1)
        outputs_and_state(h, cur)


def _mlstm(q, k, vt, gates, gate):
    B, S, _ = q.shape
    L = ML_CHUNK
    row = lambda b, c: (b, c, 0)
    return pl.pallas_call(
        _mlstm_kernel,
        grid=(B, S // L),
        in_specs=[
            pl.BlockSpec((None, L, ML_QK_WIDTH), row),
            pl.BlockSpec((None, L, ML_QK_WIDTH), row),
            pl.BlockSpec((None, ML_HEADS * ML_VT_ROWS, L), lambda b, c: (b, 0, c)),
            pl.BlockSpec((None, L, LANES), row),
            pl.BlockSpec((None, L, ML_WIDTH), row),
        ],
        out_specs=pl.BlockSpec((None, L, ML_WIDTH), row),
        out_shape=jax.ShapeDtypeStruct((B, S, ML_WIDTH), BF16),
        scratch_shapes=[pltpu.VMEM((ML_HEADS, ML_VT_ROWS, ML_QK_DIM), F32),
                        pltpu.VMEM((1, LANES), F32)],
        compiler_params=pltpu.CompilerParams(
            dimension_semantics=("arbitrary", "arbitrary"), vmem_limit_bytes=VMEM_LIMIT),
        name="mlstm",
    )(q, k, vt, gates, gate)


def _final_proj_kernel(x_ref, y_ref, wo_ref, g_ref, o_ref):
    x2 = x_ref[...] + jnp.dot(y_ref[...], wo_ref[...], preferred_element_type=F32)
    o_ref[...] = _rms(x2, g_ref[...])


def _final_proj(x1, y, wo, g):
    B, S, _ = x1.shape
    T = FINAL_BLOCK
    row = lambda b, i: (b, i, 0)
    const2 = lambda b, i: (0, 0)
    return pl.pallas_call(
        _final_proj_kernel,
        grid=(B, S // T),
        in_specs=[
            pl.BlockSpec((None, T, D_MODEL), row),
            pl.BlockSpec((None, T, ML_WIDTH), row),
            pl.BlockSpec((ML_WIDTH, D_MODEL), const2),
            pl.BlockSpec((1, D_MODEL), const2),
        ],
        out_specs=pl.BlockSpec((None, T, D_MODEL), row),
        out_shape=jax.ShapeDtypeStruct((B, S, D_MODEL), F32),
        compiler_params=pltpu.CompilerParams(
            dimension_semantics=("arbitrary", "arbitrary"), vmem_limit_bytes=VMEM_LIMIT),
        name="final_proj",
    )(x1, y, wo, g)


def _rope_lane_perm():
    n = np.arange(LANES)
    half, c, r = n // 64, (n % 64) // 32, n % 32
    per_head = c * HEAD_DIM + half * 32 + r
    return np.concatenate([h * LANES + per_head for h in range(DIFF_HEADS)])


def _pad_lanes(t):
    return jnp.pad(t, ((0, 0), (0, LANES - t.shape[1])))


def kernel(x, positions, attn_norm_g, attn_w_in, fox_b_f, diff_lam_q1, diff_lam_k1, diff_lam_q2,
           diff_lam_k2, diff_subln_g, attn_w_out, ml_norm_g, ml_w_in, ml_b_i, ml_b_f, ml_w_out,
           final_norm_g):
    B, S, _ = x.shape
    scale = HEAD_DIM ** -0.5 * LOG2E

    w = attn_w_in[0]
    o = np.cumsum((0, FOX_WIDTH, FOX_WIDTH, FOX_WIDTH, FOX_HEADS, DIFF_WIDTH, DIFF_WIDTH, DIFF_WIDTH,
                   ATTN_WIDTH))
    perm = _rope_lane_perm()
    w_main = jnp.concatenate([
        w[:, o[0]:o[1]] * scale, w[:, o[1]:o[2]], w[:, o[2]:o[3]],
        w[:, o[4]:o[5]][:, perm] * scale, w[:, o[5]:o[6]][:, perm], w[:, o[6]:o[7]],
        w[:, o[7]:o[8]]], axis=1).astype(BF16)
    w_f = _pad_lanes(w[:, o[3]:o[4]]).astype(BF16)
    b_f = _pad_lanes(fox_b_f[0][None, :])
    half = HEAD_DIM // 2
    inv = ROPE_THETA ** (-jnp.arange(half, dtype=F32) / half)
    inv = jnp.tile(inv, LANES // half)[None, :]
    posf = positions.astype(F32)[..., None]

    qf, kf, vft, fcum, qd, kd, vdt, sz = _attn_proj(
        x, posf, attn_norm_g[0][None, :], w_main, w_f, b_f, inv)

    y_fox = _attention(qf, kf, vft, sz, fox=True, f=fcum, sz_offset=0)
    lam_vecs = jnp.stack([diff_lam_q1[0], diff_lam_k1[0], diff_lam_q2[0], diff_lam_k2[0]])
    lam_init = 0.8 - 0.6 * math.exp(-0.3 * 0)
    y_diff = _attention(qd, kd, vdt, sz, fox=False, lam_vecs=lam_vecs,
                        subln_g=diff_subln_g[0][:, None], lam_init=lam_init, sz_offset=N_UNITS)

    w2 = ml_w_in[0]
    o2 = np.cumsum((0, ML_QK_WIDTH, ML_QK_WIDTH, ML_WIDTH, ML_HEADS, ML_HEADS, ML_WIDTH, ML_WIDTH))
    w2_main = jnp.concatenate([
        w2[:, o2[0]:o2[1]], w2[:, o2[1]:o2[2]] * (ML_QK_DIM ** -0.5), w2[:, o2[2]:o2[3]],
        w2[:, o2[5]:o2[6]], w2[:, o2[6]:o2[7]]], axis=1).astype(BF16)
    w2_g = _pad_lanes(w2[:, o2[3]:o2[5]]).astype(BF16)
    b_g = _pad_lanes(jnp.concatenate([ml_b_i[0], ml_b_f[0]])[None, :])

    x1, q2, k2, v2t, gates, gate = _mid_proj(
        x, y_fox, y_diff, attn_w_out[0].astype(BF16), ml_norm_g[0][None, :], w2_main, w2_g, b_g)
    y2 = _mlstm(q2, k2, v2t, gates, gate)
    return _final_proj(x1, y2, ml_w_out[0].astype(BF16), final_norm_g[None, :])
```

```python
import functools
import math

import numpy as np
import jax
import jax.numpy as jnp
from jax import lax
from jax.experimental import pallas as pl
from jax.experimental.pallas import tpu as pltpu

F32 = jnp.float32
BF16 = jnp.bfloat16

D_MODEL = 1024
HEAD_DIM = 64
FOX_HEADS = 8
DIFF_HEADS = 4
FOX_WIDTH = FOX_HEADS * HEAD_DIM
DIFF_WIDTH = DIFF_HEADS * 2 * HEAD_DIM
ATTN_WIDTH = FOX_WIDTH + DIFF_WIDTH
ML_HEADS = 8
ML_QK_DIM = 64
ML_V_DIM = 128
ML_QK_WIDTH = ML_HEADS * ML_QK_DIM
ML_WIDTH = ML_HEADS * ML_V_DIM
ROPE_THETA = 10000.0
EPS = 1e-6

LANES = 128
N_UNITS = 4
ATTN_BLOCK = 512
Q_BLOCK = 2048
Q_SUB = 256
ONES_ROWS = 16
FOX_MAP_ROWS = HEAD_DIM + ONES_ROWS
DIFF_MAP_ROWS = 2 * HEAD_DIM + ONES_ROWS
LOG2E = math.log2(math.e)
ML_CHUNK = 256
ML_VT_ROWS = ML_V_DIM + ONES_ROWS
ROW_BLOCK = 1024
ROW_PIECE = 256
FINAL_BLOCK = 1024
VMEM_LIMIT = 56 * 1024 * 1024

NEG_INF = float("-inf")


def _split3(x):
    hi = x.astype(BF16)
    r1 = x - hi.astype(F32)
    mid = r1.astype(BF16)
    lo = (r1 - mid.astype(F32)).astype(BF16)
    return hi, mid, lo


def _tri_cumsum(x):
    n = x.shape[0]
    row = lax.broadcasted_iota(jnp.int32, (n, n), 0)
    col = lax.broadcasted_iota(jnp.int32, (n, n), 1)
    tri = jnp.where(row >= col, 1.0, 0.0).astype(BF16)
    hi, mid, lo = _split3(x)
    return (jnp.dot(tri, hi, preferred_element_type=F32)
            + jnp.dot(tri, mid, preferred_element_type=F32)
            + jnp.dot(tri, lo, preferred_element_type=F32))


def _log_sigmoid(x):
    return jnp.minimum(x, 0.0) - jnp.log(1.0 + jnp.exp(-jnp.abs(x)))


def _sigmoid(x):
    return 1.0 / (1.0 + jnp.exp(-x))


def _rms(x, g):
    return x * lax.rsqrt(jnp.mean(x * x, axis=-1, keepdims=True) + EPS) * g


def _attn_proj_kernel(x_ref, pos_ref, g_ref, w_ref, wf_ref, bf_ref, inv_ref,
                      qf_ref, kf_ref, vft_ref, f_ref, qd_ref, kd_ref, vdt_ref, sz_ref,
                      carry_ref):
    @pl.when(pl.program_id(1) == 0)
    def _():
        carry_ref[...] = jnp.zeros_like(carry_ref)

    H = ROW_PIECE
    pieces = [slice(i * H, (i + 1) * H) for i in range(ROW_BLOCK // H)]
    hbs = [_rms(x_ref[r, :], g_ref[...]).astype(BF16) for r in pieces]
    lane = lax.broadcasted_iota(jnp.int32, (1, LANES), 1)
    ones = jnp.ones((ONES_ROWS, H), BF16)

    for r, hb in zip(pieces, hbs):
        kb = r.start // ATTN_BLOCK
        kl = slice(r.start % ATTN_BLOCK, r.start % ATTN_BLOCK + H)

        def proj(lo, hi):
            return jnp.dot(hb, w_ref[:, lo:hi], preferred_element_type=F32)

        qf_ref[r, :] = proj(0, 512).astype(BF16)
        kf_ref[r, :] = proj(512, 1024).astype(BF16)
        vf = proj(1024, 1536)
        for u in range(N_UNITS):
            t = vf[:, u * LANES:(u + 1) * LANES].T.astype(BF16)
            for c in range(2):
                base = c * FOX_MAP_ROWS
                vft_ref[u, kb, base:base + HEAD_DIM, kl] = t[c * HEAD_DIM:(c + 1) * HEAD_DIM, :]
                vft_ref[u, kb, base + HEAD_DIM:base + FOX_MAP_ROWS, kl] = ones

        ang = pos_ref[r, :] * inv_ref[...]
        cos = jnp.cos(ang)
        sin = jnp.sin(ang)
        sin_signed = jnp.where(lane < 64, -sin, sin)

        def rope_store(lo, out_ref):
            t = proj(lo, lo + 512)
            for h in range(DIFF_HEADS):
                xh = t[:, h * LANES:(h + 1) * LANES]
                out_ref[r, h * LANES:(h + 1) * LANES] = (
                    xh * cos + pltpu.roll(xh, 64, 1) * sin_signed).astype(BF16)

        rope_store(1536, qd_ref)
        rope_store(2048, kd_ref)
        vd = proj(2560, 3072)
        for u in range(N_UNITS):
            vdt_ref[u, kb, 0:LANES, kl] = vd[:, u * LANES:(u + 1) * LANES].T.astype(BF16)
            vdt_ref[u, kb, LANES:DIFF_MAP_ROWS, kl] = ones

        z = proj(3072, 4096)
        sz_ref[r, :] = (z * _sigmoid(z)).astype(BF16)

        lf = _log_sigmoid(jnp.dot(hb, wf_ref[...], preferred_element_type=F32) + bf_ref[...])
        cs = _tri_cumsum(lf) + carry_ref[...]
        carry_ref[...] = cs[H - 1:H, :]
        for u in range(N_UNITS):
            f_ref[u, r, :] = cs[:, 2 * u:2 * u + 2] * LOG2E


def _attn_proj(x, posf, g, w, wf, bf, inv):
    B, S, _ = x.shape
    T = ROW_BLOCK
    nb = S // ATTN_BLOCK
    row = lambda b, i: (b, i, 0)
    const2 = lambda b, i: (0, 0)
    vt_spec = lambda rows: pl.BlockSpec((None, N_UNITS, T // ATTN_BLOCK, rows, ATTN_BLOCK),
                                        lambda b, i: (b, 0, i, 0, 0))
    out_shape = (
        jax.ShapeDtypeStruct((B, S, 512), BF16),
        jax.ShapeDtypeStruct((B, S, 512), BF16),
        jax.ShapeDtypeStruct((B, N_UNITS, nb, 2 * FOX_MAP_ROWS, ATTN_BLOCK), BF16),
        jax.ShapeDtypeStruct((B, N_UNITS, S, 2), F32),
        jax.ShapeDtypeStruct((B, S, 512), BF16),
        jax.ShapeDtypeStruct((B, S, 512), BF16),
        jax.ShapeDtypeStruct((B, N_UNITS, nb, DIFF_MAP_ROWS, ATTN_BLOCK), BF16),
        jax.ShapeDtypeStruct((B, S, ATTN_WIDTH), BF16),
    )
    return pl.pallas_call(
        _attn_proj_kernel,
        grid=(B, S // T),
        in_specs=[
            pl.BlockSpec((None, T, D_MODEL), row),
            pl.BlockSpec((None, T, 1), row),
            pl.BlockSpec((1, D_MODEL), const2),
            pl.BlockSpec((D_MODEL, 4096), const2),
            pl.BlockSpec((D_MODEL, LANES), const2),
            pl.BlockSpec((1, LANES), const2),
            pl.BlockSpec((1, LANES), const2),
        ],
        out_specs=(
            pl.BlockSpec((None, T, 512), row),
            pl.BlockSpec((None, T, 512), row),
            vt_spec(2 * FOX_MAP_ROWS),
            pl.BlockSpec((None, N_UNITS, T, 2), lambda b, i: (b, 0, i, 0)),
            pl.BlockSpec((None, T, 512), row),
            pl.BlockSpec((None, T, 512), row),
            vt_spec(DIFF_MAP_ROWS),
            pl.BlockSpec((None, T, ATTN_WIDTH), row),
        ),
        out_shape=out_shape,
        scratch_shapes=[pltpu.VMEM((1, LANES), F32)],
        compiler_params=pltpu.CompilerParams(
            dimension_semantics=("arbitrary", "arbitrary"), vmem_limit_bytes=VMEM_LIMIT),
        name="attn_proj",
    )(x, posf, g, w, wf, bf, inv)


def _attention_kernel(*refs, fox, lam_init):
    if fox:
        q_ref, k_ref, vt_ref, f_ref, sz_ref, o_ref, acc_ref, sa_ref, sb_ref = refs
    else:
        q_ref, k_ref, vt_ref, lam_ref, g_ref, sz_ref, o_ref, acc_ref, sa_ref, sb_ref = refs
    T = ATTN_BLOCK
    n_sub = Q_BLOCK // Q_SUB
    vrows = FOX_MAP_ROWS if fox else DIFF_MAP_ROWS
    vdim = vrows - ONES_ROWS
    qi = pl.program_id(2)
    kb_diag = (Q_BLOCK // T) * qi

    q = q_ref[...]
    lane = lax.broadcasted_iota(jnp.int32, (1, LANES), 1)
    first = (lane < 64) if fox else ((lane % 64) < 32)
    zero = jnp.zeros_like(q)
    qm = (jnp.where(first, q, zero), jnp.where(first, zero, q))
    qm = [[qc[j * Q_SUB:(j + 1) * Q_SUB, :] for j in range(n_sub)] for qc in qm]

    acc_ref[...] = jnp.zeros_like(acc_ref)
    if fox:
        q0 = pl.multiple_of(qi * Q_BLOCK, Q_BLOCK)
        f0 = [f_ref[pl.ds(q0, 1), c:c + 1] for c in range(2)]

    chains = [(c, j) for c in range(2) for j in range(n_sub)]
    n_diag = Q_BLOCK // T
    per_blk = n_sub // n_diag

    def scores(kb, s_ref, diag=None):
        start = pl.multiple_of(kb * T, T)
        k = k_ref[pl.ds(start, T), :]
        if fox:
            bias = [f_ref[pl.ds(start, T), c:c + 1] - f0[c] for c in range(2)]
        if diag is not None:
            kpos = lax.broadcasted_iota(jnp.int32, (T, Q_SUB), 0)
            qpos = lax.broadcasted_iota(jnp.int32, (T, Q_SUB), 1)
        for c, j in chains:
            if diag is not None and j // per_blk < diag:
                yield None
                continue
            sc = lax.dot_general(k, qm[c][j], (((1,), (1,)), ((), ())),
                                 preferred_element_type=F32)
            if fox:
                sc = sc - bias[c]
            if diag is not None and j // per_blk == diag:
                sc = jnp.where(kpos <= qpos + (j % per_blk) * Q_SUB, sc, NEG_INF)
            s_ref[c, :, j * Q_SUB:(j + 1) * Q_SUB] = sc
            yield jnp.max(sc, axis=0, keepdims=True)

    def accumulate(kb, s_ref, mb, m):
        for i, (c, j) in enumerate(chains):
            if mb[i] is None:
                yield m[i]
                continue
            lanes = slice(j * Q_SUB, (j + 1) * Q_SUB)
            m_new = jnp.maximum(m[i], mb[i])
            alpha = jnp.exp2(m[i] - m_new)
            pb = jnp.exp2(s_ref[c, :, lanes] - m_new).astype(BF16)
            vt = vt_ref[kb, 0:vrows, :] if not fox else vt_ref[kb, c * vrows:(c + 1) * vrows, :]
            acc_ref[c, :, lanes] = alpha * acc_ref[c, :, lanes] + jnp.dot(
                vt, pb, preferred_element_type=F32)
            yield m_new

    def overlapped(next_scores, kb_cur, s_cur, mb_cur, m):
        mb_next, m_new = [], []
        for a, b in zip(next_scores, accumulate(kb_cur, s_cur, mb_cur, m)):
            mb_next.append(a)
            m_new.append(b)
        return tuple(mb_next), tuple(m_new)

    bufs = (sa_ref, sb_ref)
    m = (jnp.full((1, Q_SUB), NEG_INF, F32),) * len(chains)
    mb = tuple(scores(kb_diag + n_diag - 1, bufs[0], diag=n_diag - 1))
    for i in range(1, n_diag):
        d = n_diag - 1 - i
        mb, m = overlapped(scores(kb_diag + d, bufs[i % 2], diag=d),
                           kb_diag + d + 1, bufs[(i - 1) % 2], mb, m)

    def pair(jj, carry):
        mb_b, m = carry
        kb_b = jnp.where(jj == 0, kb_diag, 2 * jj - 1)
        mb_a, m = overlapped(scores(2 * jj, sa_ref), kb_b, sb_ref, mb_b, m)
        mb_b, m = overlapped(scores(2 * jj + 1, sb_ref), 2 * jj, sa_ref, mb_a, m)
        return mb_b, m

    mb, m = lax.fori_loop(0, kb_diag // 2, pair, (mb, m))
    tuple(accumulate(jnp.where(qi == 0, kb_diag, kb_diag - 1), sb_ref, mb, m))

    sz = sz_ref[...].astype(F32)
    o0 = acc_ref[0, 0:vdim, :] / acc_ref[0, vdim:vdim + 1, :]
    o1 = acc_ref[1, 0:vdim, :] / acc_ref[1, vdim:vdim + 1, :]
    if fox:
        o_t = jnp.concatenate([o0, o1], axis=0)
        o_ref[...] = (o_t.T * sz).astype(BF16)
    else:
        lam = (jnp.exp(jnp.sum(lam_ref[0:1, :] * lam_ref[1:2, :], axis=1, keepdims=True))
               - jnp.exp(jnp.sum(lam_ref[2:3, :] * lam_ref[3:4, :], axis=1, keepdims=True))
               + lam_init)
        o_t = o0 - lam * o1
        var = jnp.mean(o_t * o_t, axis=0, keepdims=True)
        y_t = o_t * lax.rsqrt(var + EPS) * g_ref[...] * (1.0 - lam_init)
        o_ref[...] = (y_t.T * sz).astype(BF16)


def _attention(q, k, vt, sz, *, fox, f=None, lam_vecs=None, subln_g=None, lam_init=0.0, sz_offset=0):
    B, S, _ = q.shape
    T = ATTN_BLOCK
    nk = S // T
    nq = S // Q_BLOCK
    qspec = pl.BlockSpec((None, Q_BLOCK, LANES), lambda b, u, i: (b, i, u))
    kspec = pl.BlockSpec((None, S, LANES), lambda b, u, i: (b, 0, u))
    vspec = pl.BlockSpec((None, None, nk, vt.shape[3], T), lambda b, u, i: (b, u, 0, 0, 0))
    szspec = pl.BlockSpec((None, Q_BLOCK, LANES), lambda b, u, i: (b, i, u + sz_offset))
    if fox:
        extra = [f]
        extra_specs = [pl.BlockSpec((None, None, S, 2), lambda b, u, i: (b, u, 0, 0),
                                    pipeline_mode=pl.Buffered(1))]
        scratch = [pltpu.VMEM((2, FOX_MAP_ROWS, Q_BLOCK), F32)]
    else:
        extra = [lam_vecs, subln_g]
        extra_specs = [pl.BlockSpec((4, HEAD_DIM), lambda b, u, i: (0, 0)),
                       pl.BlockSpec((LANES, 1), lambda b, u, i: (0, 0))]
        scratch = [pltpu.VMEM((2, DIFF_MAP_ROWS, Q_BLOCK), F32)]
    return pl.pallas_call(
        functools.partial(_attention_kernel, fox=fox, lam_init=lam_init),
        grid=(B, N_UNITS, nq),
        in_specs=[qspec, kspec, vspec] + extra_specs + [szspec],
        out_specs=pl.BlockSpec((None, Q_BLOCK, LANES), lambda b, u, i: (b, i, u)),
        out_shape=jax.ShapeDtypeStruct((B, S, N_UNITS * LANES), BF16),
        scratch_shapes=scratch + [pltpu.VMEM((2, T, Q_BLOCK), F32)] * 2,
        compiler_params=pltpu.CompilerParams(
            dimension_semantics=("arbitrary", "arbitrary", "arbitrary"),
            vmem_limit_bytes=VMEM_LIMIT),
        name="fox_attention" if fox else "diff_attention",
    )(q, k, vt, *extra, sz)


def _mid_proj_kernel(x_ref, yf_ref, yd_ref, wo_ref, g_ref, w_ref, wg_ref, bg_ref,
                     x1_ref, q_ref, k_ref, vt_ref, gates_ref, gate_ref):
    H = ROW_PIECE
    pieces = [slice(i * H, (i + 1) * H) for i in range(ROW_BLOCK // H)]
    x1 = []
    for r in pieces:
        x1.append(x_ref[r, :]
                  + jnp.dot(yf_ref[r, :], wo_ref[0:FOX_WIDTH, :], preferred_element_type=F32)
                  + jnp.dot(yd_ref[r, :], wo_ref[FOX_WIDTH:ATTN_WIDTH, :], preferred_element_type=F32))
    lane = lax.broadcasted_iota(jnp.int32, (1, LANES), 1)
    ones = jnp.ones((ONES_ROWS, H), BF16)
    for r, xh in zip(pieces, x1):
        x1_ref[r, :] = xh
        hb = _rms(xh, g_ref[...]).astype(BF16)

        def proj(lo, hi):
            return jnp.dot(hb, w_ref[:, lo:hi], preferred_element_type=F32)

        q_ref[r, :] = proj(0, 512).astype(BF16)
        k_ref[r, :] = proj(512, 1024).astype(BF16)
        v = proj(1024, 2048)
        for h in range(ML_HEADS):
            base = h * ML_VT_ROWS
            vt_ref[base:base + ML_V_DIM, r] = v[:, h * ML_V_DIM:(h + 1) * ML_V_DIM].T.astype(BF16)
            vt_ref[base + ML_V_DIM:base + ML_VT_ROWS, r] = ones
        og = proj(2048, 3072)
        z = proj(3072, 4096)
        gate_ref[r, :] = (_sigmoid(og) * (z * _sigmoid(z))).astype(BF16)

        gt = jnp.dot(hb, wg_ref[...], preferred_element_type=F32) + bg_ref[...]
        gates_ref[r, :] = jnp.where(lane < ML_HEADS, gt, _log_sigmoid(gt))


def _mid_proj(x, yf, yd, wo, g, w, wg, bg):
    B, S, _ = x.shape
    T = ROW_BLOCK
    row = lambda b, i: (b, i, 0)
    const2 = lambda b, i: (0, 0)
    return pl.pallas_call(
        _mid_proj_kernel,
        grid=(B, S // T),
        in_specs=[
            pl.BlockSpec((None, T, D_MODEL), row),
            pl.BlockSpec((None, T, FOX_WIDTH), row),
            pl.BlockSpec((None, T, DIFF_WIDTH), row),
            pl.BlockSpec((ATTN_WIDTH, D_MODEL), const2),
            pl.BlockSpec((1, D_MODEL), const2),
            pl.BlockSpec((D_MODEL, 4096), const2),
            pl.BlockSpec((D_MODEL, LANES), const2),
            pl.BlockSpec((1, LANES), const2),
        ],
        out_specs=(
            pl.BlockSpec((None, T, D_MODEL), row),
            pl.BlockSpec((None, T, ML_QK_WIDTH), row),
            pl.BlockSpec((None, T, ML_QK_WIDTH), row),
            pl.BlockSpec((None, ML_HEADS * ML_VT_ROWS, T), lambda b, i: (b, 0, i)),
            pl.BlockSpec((None, T, LANES), row),
            pl.BlockSpec((None, T, ML_WIDTH), row),
        ),
        out_shape=(
            jax.ShapeDtypeStruct((B, S, D_MODEL), F32),
            jax.ShapeDtypeStruct((B, S, ML_QK_WIDTH), BF16),
            jax.ShapeDtypeStruct((B, S, ML_QK_WIDTH), BF16),
            jax.ShapeDtypeStruct((B, ML_HEADS * ML_VT_ROWS, S), BF16),
            jax.ShapeDtypeStruct((B, S, LANES), F32),
            jax.ShapeDtypeStruct((B, S, ML_WIDTH), BF16),
        ),
        compiler_params=pltpu.CompilerParams(
            dimension_semantics=("arbitrary", "arbitrary"), vmem_limit_bytes=VMEM_LIMIT),
        name="mid_proj",
    )(x, yf, yd, wo, g, w, wg, bg)


def _mlstm_kernel(q_ref, k_ref, vt_ref, gates_ref, gate_ref, o_ref, ct_ref, m_ref):
    L = ML_CHUNK
    NT = (((1,), (1,)), ((), ()))

    @pl.when(pl.program_id(1) == 0)
    def _():
        ct_ref[...] = jnp.zeros_like(ct_ref)
        m_ref[...] = jnp.zeros_like(m_ref)

    gts = gates_ref[...]
    lane = lax.broadcasted_iota(jnp.int32, (1, LANES), 1)
    cols = jnp.where(lane < ML_HEADS, gts, _tri_cumsum(gts))
    rows = cols.T
    allowed = (lax.broadcasted_iota(jnp.int32, (L, L), 0)
               <= lax.broadcasted_iota(jnp.int32, (L, L), 1))

    def decay_and_scores(h):
        li_row = rows[h:h + 1, :]
        b_row = rows[ML_HEADS + h:ML_HEADS + h + 1, :]
        u_col = cols[:, h:h + 1] - cols[:, ML_HEADS + h:ML_HEADS + h + 1]
        a = b_row[:, L - 1:L]
        m_prev = m_ref[:, h:h + 1]
        log_d = jnp.where(allowed, u_col + b_row, NEG_INF)
        log_inter = b_row + m_prev
        m_t = jnp.maximum(log_inter, jnp.max(log_d, axis=0, keepdims=True))
        d = jnp.exp(log_d - m_t)
        w_inter = jnp.exp(log_inter - m_t)
        qh = q_ref[:, h * ML_QK_DIM:(h + 1) * ML_QK_DIM]
        kh = k_ref[:, h * ML_QK_DIM:(h + 1) * ML_QK_DIM]
        ct = ct_ref[h]
        kq = lax.dot_general(kh, qh, NT, preferred_element_type=F32)
        ctq = lax.dot_general(ct.astype(BF16), qh, NT, preferred_element_type=F32)
        return dict(li_row=li_row, b_row=b_row, a=a, m_prev=m_prev, m_t=m_t, d=d,
                    w_inter=w_inter, kh=kh, ct=ct, kq=kq, ctq=ctq)

    def outputs_and_state(h, st):
        vt = vt_ref[h * ML_VT_ROWS:(h + 1) * ML_VT_ROWS, :]
        s = (st["kq"] * st["d"]).astype(BF16)
        tot = st["w_inter"] * st["ctq"] + jnp.dot(vt, s, preferred_element_type=F32)
        den = tot[ML_V_DIM:ML_V_DIM + 1, :]
        h_t = tot[0:ML_V_DIM, :] / jnp.maximum(jnp.abs(den), jnp.exp(-st["m_t"]))
        sl = slice(h * ML_V_DIM, (h + 1) * ML_V_DIM)
        o_ref[:, sl] = (h_t.T * gate_ref[:, sl].astype(F32)).astype(BF16)

        g_row = st["a"] - st["b_row"] + st["li_row"]
        m_new = jnp.maximum(st["a"] + st["m_prev"], jnp.max(g_row, axis=1, keepdims=True))
        w_old = jnp.exp(st["a"] + st["m_prev"] - m_new)
        vw = (vt.astype(F32) * jnp.exp(g_row - m_new)).astype(BF16)
        ct_ref[h] = w_old * st["ct"] + jnp.dot(vw, st["kh"], preferred_element_type=F32)
        m_ref[:, h:h + 1] = m_new

    ahead = decay_and_scores(0)
    for h in range(ML_HEADS):
        cur = ahead
        if h + 1 < ML_HEADS:
            ahead = decay_and_scores(h + 1)
        outputs_and_state(h, cur)


def _mlstm(q, k, vt, gates, gate):
    B, S, _ = q.shape
    L = ML_CHUNK
    row = lambda b, c: (b, c, 0)
    return pl.pallas_call(
        _mlstm_kernel,
        grid=(B, S // L),
        in_specs=[
            pl.BlockSpec((None, L, ML_QK_WIDTH), row),
            pl.BlockSpec((None, L, ML_QK_WIDTH), row),
            pl.BlockSpec((None, ML_HEADS * ML_VT_ROWS, L), lambda b, c: (b, 0, c)),
            pl.BlockSpec((None, L, LANES), row),
            pl.BlockSpec((None, L, ML_WIDTH), row),
        ],
        out_specs=pl.BlockSpec((None, L, ML_WIDTH), row),
        out_shape=jax.ShapeDtypeStruct((B, S, ML_WIDTH), BF16),
        scratch_shapes=[pltpu.VMEM((ML_HEADS, ML_VT_ROWS, ML_QK_DIM), F32),
                        pltpu.VMEM((1, LANES), F32)],
        compiler_params=pltpu.CompilerParams(
            dimension_semantics=("arbitrary", "arbitrary"), vmem_limit_bytes=VMEM_LIMIT),
        name="mlstm",
    )(q, k, vt, gates, gate)


def _final_proj_kernel(x_ref, y_ref, wo_ref, g_ref, o_ref):
    x2 = x_ref[...] + jnp.dot(y_ref[...], wo_ref[...], preferred_element_type=F32)
    o_ref[...] = _rms(x2, g_ref[...])


def _final_proj(x1, y, wo, g):
    B, S, _ = x1.shape
    T = FINAL_BLOCK
    row = lambda b, i: (b, i, 0)
    const2 = lambda b, i: (0, 0)
    return pl.pallas_call(
        _final_proj_kernel,
        grid=(B, S // T),
        in_specs=[
            pl.BlockSpec((None, T, D_MODEL), row),
            pl.BlockSpec((None, T, ML_WIDTH), row),
            pl.BlockSpec((ML_WIDTH, D_MODEL), const2),
            pl.BlockSpec((1, D_MODEL), const2),
        ],
        out_specs=pl.BlockSpec((None, T, D_MODEL), row),
        out_shape=jax.ShapeDtypeStruct((B, S, D_MODEL), F32),
        compiler_params=pltpu.CompilerParams(
            dimension_semantics=("arbitrary", "arbitrary"), vmem_limit_bytes=VMEM_LIMIT),
        name="final_proj",
    )(x1, y, wo, g)


def _rope_lane_perm():
    n = np.arange(LANES)
    half, c, r = n // 64, (n % 64) // 32, n % 32
    per_head = c * HEAD_DIM + half * 32 + r
    return np.concatenate([h * LANES + per_head for h in range(DIFF_HEADS)])


def _pad_lanes(t):
    return jnp.pad(t, ((0, 0), (0, LANES - t.shape[1])))


def kernel(x, positions, attn_norm_g, attn_w_in, fox_b_f, diff_lam_q1, diff_lam_k1, diff_lam_q2,
           diff_lam_k2, diff_subln_g, attn_w_out, ml_norm_g, ml_w_in, ml_b_i, ml_b_f, ml_w_out,
           final_norm_g):
    B, S, _ = x.shape
    scale = HEAD_DIM ** -0.5 * LOG2E

    w = attn_w_in[0]
    o = np.cumsum((0, FOX_WIDTH, FOX_WIDTH, FOX_WIDTH, FOX_HEADS, DIFF_WIDTH, DIFF_WIDTH, DIFF_WIDTH,
                   ATTN_WIDTH))
    perm = _rope_lane_perm()
    w_main = jnp.concatenate([
        w[:, o[0]:o[1]] * scale, w[:, o[1]:o[2]], w[:, o[2]:o[3]],
        w[:, o[4]:o[5]][:, perm] * scale, w[:, o[5]:o[6]][:, perm], w[:, o[6]:o[7]],
        w[:, o[7]:o[8]]], axis=1).astype(BF16)
    w_f = _pad_lanes(w[:, o[3]:o[4]]).astype(BF16)
    b_f = _pad_lanes(fox_b_f[0][None, :])
    half = HEAD_DIM // 2
    inv = ROPE_THETA ** (-jnp.arange(half, dtype=F32) / half)
    inv = jnp.tile(inv, LANES // half)[None, :]
    posf = positions.astype(F32)[..., None]

    qf, kf, vft, fcum, qd, kd, vdt, sz = _attn_proj(
        x, posf, attn_norm_g[0][None, :], w_main, w_f, b_f, inv)

    y_fox = _attention(qf, kf, vft, sz, fox=True, f=fcum, sz_offset=0)
    lam_vecs = jnp.stack([diff_lam_q1[0], diff_lam_k1[0], diff_lam_q2[0], diff_lam_k2[0]])
    lam_init = 0.8 - 0.6 * math.exp(-0.3 * 0)
    y_diff = _attention(qd, kd, vdt, sz, fox=False, lam_vecs=lam_vecs,
                        subln_g=diff_subln_g[0][:, None], lam_init=lam_init, sz_offset=N_UNITS)

    w2 = ml_w_in[0]
    o2 = np.cumsum((0, ML_QK_WIDTH, ML_QK_WIDTH, ML_WIDTH, ML_HEADS, ML_HEADS, ML_WIDTH, ML_WIDTH))
    w2_main = jnp.concatenate([
        w2[:, o2[0]:o2[1]], w2[:, o2[1]:o2[2]] * (ML_QK_DIM ** -0.5), w2[:, o2[2]:o2[3]],
        w2[:, o2[5]:o2[6]], w2[:, o2[6]:o2[7]]], axis=1).astype(BF16)
    w2_g = _pad_lanes(w2[:, o2[3]:o2[5]]).astype(BF16)
    b_g = _pad_lanes(jnp.concatenate([ml_b_i[0], ml_b_f[0]])[None, :])

    x1, q2, k2, v2t, gates, gate = _mid_proj(
        x, y_fox, y_diff, attn_w_out[0].astype(BF16), ml_norm_g[0][None, :], w2_main, w2_g, b_g)
    y2 = _mlstm(q2, k2, v2t, gates, gate)
    return _final_proj(x1, y2, ml_w_out[0].astype(BF16), final_norm_g[None, :])
```
